```python
import math
import jax, jax.numpy as jnp
from jax import lax
import numpy as np

D_MODEL = 1024
BATCH = 2
SEQ = 8192
DEPTH = 2

GRID_W = 64
CTX_LEN = 256
HEAD_DIM = 64
NA_HEADS = 4
NA_WIDTH = NA_HEADS * HEAD_DIM
WIN_H = 8
WIN_W = 16
NA_QB = WIN_W
NA_KBW = 2 * WIN_W
DIFF_HEADS = 4
DIFF_QK_WIDTH = 2 * DIFF_HEADS * HEAD_DIM
DIFF_WIDTH = DIFF_HEADS * 2 * HEAD_DIM
CONV_WIDTH = 256
CONV_K = 3
MIX_WIDTH = NA_WIDTH + DIFF_WIDTH + CONV_WIDTH
OFF_K = NA_WIDTH + DIFF_QK_WIDTH
OFF_V = OFF_K + NA_WIDTH + DIFF_QK_WIDTH
OFF_CONV = OFF_V + NA_WIDTH + DIFF_WIDTH
IN_COLS = OFF_CONV + 3 * CONV_WIDTH
SPLIT_IDX = (NA_WIDTH, OFF_K, OFF_K + NA_WIDTH, OFF_V, OFF_V + NA_WIDTH, OFF_CONV, OFF_CONV + CONV_WIDTH, OFF_CONV + 2 * CONV_WIDTH)
KV_SPLIT_IDX = (NA_WIDTH, NA_WIDTH + DIFF_QK_WIDTH, 2 * NA_WIDTH + DIFF_QK_WIDTH)
Q_BLOCK = 128
ROPE_BASE = 10000.0
D_FF = 2816
N_EXPERTS = 8
TOP_K = 2
D_EXPERT = 3584
N_DENSE = (DEPTH + 1) // 2
N_MOE = DEPTH // 2
EPS = 1e-6
NEG_INF = -1e30

kernel_name = 'hybrid_na_diffattn_shortconv_moe_dit'


def rms_norm(x, gain=None):
    xf = x.astype(jnp.float32)
    y = xf * lax.rsqrt(jnp.mean(jnp.square(xf), axis=-1, keepdims=True) + EPS)
    if gain is not None:
        y = y * gain.astype(jnp.float32)
    return y.astype(x.dtype)


def modulate(h, shift, scale):
    return h * (1 + scale) + shift


def axial_rope(n, dtype):
    t = jnp.arange(n, dtype=jnp.int32)
    row = (t // GRID_W).astype(jnp.float32)
    col = (t % GRID_W).astype(jnp.float32)
    n_freq = HEAD_DIM // 4
    inv_freq = ROPE_BASE ** (-jnp.arange(n_freq, dtype=jnp.float32) / n_freq)
    ang = jnp.concatenate([row[:, None] * inv_freq, col[:, None] * inv_freq], axis=-1)
    return jnp.cos(ang).astype(dtype), jnp.sin(ang).astype(dtype)


def apply_rope(x, cos, sin):
    half = HEAD_DIM // 2
    x1, x2 = x[..., :half], x[..., half:]
    c = cos[None, :, None, :]
    s = sin[None, :, None, :]
    return jnp.concatenate([x1 * c - x2 * s, x2 * c + x1 * s], axis=-1)


def ctx_softmax_attention(q, k, v):
    s = jnp.einsum('bqhd,bkhd->bhqk', q, k, preferred_element_type=jnp.float32) * HEAD_DIM ** -0.5
    p = jax.nn.softmax(s, axis=-1).astype(v.dtype)
    return jnp.einsum('bhqk,bkhd->bqhd', p, v)


def neighbourhood_attention(q, k, v, kc, vc, rpb):
    b, n = q.shape[:2]
    rows = n // GRID_W
    kh = min(WIN_H, rows)
    ncb = GRID_W // NA_QB
    r = jnp.arange(rows)
    j = jnp.arange(ncb)
    row_start = jnp.clip(r - kh // 2, 0, rows - kh)
    col_blk = jnp.clip(j * NA_QB - WIN_W // 2, 0, GRID_W - NA_KBW)
    key_rows = row_start[:, None] + jnp.arange(kh)[None, :]
    key_cols = col_blk[:, None] + jnp.arange(NA_KBW)[None, :]
    idx = key_rows[:, None, :, None] * GRID_W + key_cols[None, :, None, :]
    nk = kh * NA_KBW
    k_blk = jnp.take(k, idx.reshape(-1), axis=1).reshape(b, rows, ncb, nk, NA_HEADS, HEAD_DIM)
    v_blk = jnp.take(v, idx.reshape(-1), axis=1).reshape(b, rows, ncb, nk, NA_HEADS, HEAD_DIM)
    q_blk = q.reshape(b, rows, ncb, NA_QB, NA_HEADS, HEAD_DIM)
    q_cols = j[:, None] * NA_QB + jnp.arange(NA_QB)[None, :]
    win_start = jnp.clip(q_cols - WIN_W // 2, 0, GRID_W - WIN_W)
    kc_b = key_cols[:, None, :]
    valid = (kc_b >= win_start[:, :, None]) & (kc_b < win_start[:, :, None] + WIN_W)
    dcol_idx = jnp.clip(kc_b - q_cols[:, :, None] + WIN_W - 1, 0, 2 * WIN_W - 2)
    drow_idx = key_rows - r[:, None] + WIN_H - 1
    bias = rpb[:, drow_idx[:, None, None, :, None], dcol_idx[None, :, :, None, :]].astype(jnp.float32)
    bias = jnp.where(valid[None, None, :, :, None, :], bias, NEG_INF)
    bias = bias.transpose(1, 2, 0, 3, 4, 5).reshape(rows, ncb, NA_HEADS, NA_QB, nk)
    scale = HEAD_DIM ** -0.5
    s_lat = jnp.einsum('brjqhd,brjkhd->brjhqk', q_blk, k_blk, preferred_element_type=jnp.float32) * scale + bias
    s_ctx = jnp.einsum('brjqhd,bkhd->brjhqk', q_blk, kc, preferred_element_type=jnp.float32) * scale
    p = jax.nn.softmax(jnp.concatenate([s_lat, s_ctx], axis=-1), axis=-1).astype(v.dtype)
    out = (jnp.einsum('brjhqk,brjkhd->brjqhd', p[..., :nk], v_blk)
           + jnp.einsum('brjhqk,bkhd->brjqhd', p[..., nk:], vc))
    return out.reshape(b, n, NA_WIDTH)


def diff_attend(q, k, v, lam):
    s = jnp.einsum('bqhd,bkhd->bhqk', q, k, preferred_element_type=jnp.float32) * HEAD_DIM ** -0.5
    p = jax.nn.softmax(s, axis=-1)
    b, _, nq, nkeys = p.shape
    p = p.reshape(b, DIFF_HEADS, 2, nq, nkeys)
    a = (p[:, :, 0] - lam * p[:, :, 1]).astype(v.dtype)
    return jnp.einsum('bhqk,bkhd->bqhd', a, v)


def diff_latent(q, k_all, v_all, lam):
    b, n = q.shape[:2]
    qb = q.reshape(b, n // Q_BLOCK, Q_BLOCK, 2 * DIFF_HEADS, HEAD_DIM).transpose(1, 0, 2, 3, 4)
    out = lax.map(lambda blk: diff_attend(blk, k_all, v_all, lam), qb)
    return out.transpose(1, 0, 2, 3, 4).reshape(b, n, DIFF_HEADS, 2 * HEAD_DIM)


def diff_post(o, subln_gain, lambda_init):
    return rms_norm(o, subln_gain) * (1.0 - lambda_init)


def short_conv(u, w):
    n = u.shape[1]
    up = jnp.pad(u, ((0, 0), (CONV_K // 2, CONV_K // 2), (0, 0)))
    y = w[0] * up[:, 0:n]
    for t in range(1, CONV_K):
        y = y + w[t] * up[:, t:t + n]
    return y


def swiglu(h, w1, w3, w2):
    return (jax.nn.silu(h @ w1) * (h @ w3)) @ w2


def moe_ffn(h, w_router, w1, w3, w2):
    logits = jnp.einsum('...d,de->...e', h.astype(jnp.float32), w_router.astype(jnp.float32))
    top_vals, top_idx = lax.top_k(logits, TOP_K)
    top_w = jax.nn.softmax(top_vals, axis=-1)
    gates = jnp.sum(jax.nn.one_hot(top_idx, N_EXPERTS, dtype=jnp.float32) * top_w[..., None], axis=-2).astype(h.dtype)
    y = jnp.zeros_like(h)
    for e in range(N_EXPERTS):
        y = y + gates[..., e:e + 1] * swiglu(h, w1[e], w3[e], w2[e])
    return y


def token_mixer(h_lat, h_ctx, w_in, w_out, rpb, lam, lambda_init, subln_gain, conv_w, cos, sin, ctx_out):
    b, n, _ = h_lat.shape
    L = h_ctx.shape[1]
    qa, qb, ka, kb, va, vb, xin, bg, cg = jnp.split(h_lat @ w_in, SPLIT_IDX, axis=-1)
    qa = qa.reshape(b, n, NA_HEADS, HEAD_DIM)
    ka = ka.reshape(b, n, NA_HEADS, HEAD_DIM)
    va = va.reshape(b, n, NA_HEADS, HEAD_DIM)
    qb = apply_rope(qb.reshape(b, n, 2 * DIFF_HEADS, HEAD_DIM), cos, sin)
    kb = apply_rope(kb.reshape(b, n, 2 * DIFF_HEADS, HEAD_DIM), cos, sin)
    vb = vb.reshape(b, n, DIFF_HEADS, 2 * HEAD_DIM)
    ka_c, kb_c, va_c, vb_c = jnp.split(h_ctx @ w_in[:, OFF_K:OFF_CONV], KV_SPLIT_IDX, axis=-1)
    ka_c = ka_c.reshape(b, L, NA_HEADS, HEAD_DIM)
    va_c = va_c.reshape(b, L, NA_HEADS, HEAD_DIM)
    kb_c = kb_c.reshape(b, L, 2 * DIFF_HEADS, HEAD_DIM)
    vb_c = vb_c.reshape(b, L, DIFF_HEADS, 2 * HEAD_DIM)
    na = neighbourhood_attention(qa, ka, va, ka_c, va_c, rpb)
    k_all = jnp.concatenate([kb_c, kb], axis=1)
    v_all = jnp.concatenate([vb_c, vb], axis=1)
    dif = diff_post(diff_latent(qb, k_all, v_all, lam), subln_gain, lambda_init).reshape(b, n, DIFF_WIDTH)
    conv = bg * short_conv(cg * xin, conv_w)
    out_lat = jnp.concatenate([na, dif, conv], axis=-1) @ w_out
    if not ctx_out:
        return out_lat, None
    qa_c, qb_c = jnp.split(h_ctx @ w_in[:, :OFF_K], (NA_WIDTH,), axis=-1)
    xin_c, bg_c, cg_c = jnp.split(h_ctx @ w_in[:, OFF_CONV:], (CONV_WIDTH, 2 * CONV_WIDTH), axis=-1)
    na_c = ctx_softmax_attention(qa_c.reshape(b, L, NA_HEADS, HEAD_DIM), ka_c, va_c).reshape(b, L, NA_WIDTH)
    dif_c = diff_post(diff_attend(qb_c.reshape(b, L, 2 * DIFF_HEADS, HEAD_DIM), kb_c, vb_c, lam),
                      subln_gain, lambda_init).reshape(b, L, DIFF_WIDTH)
    conv_c = bg_c * short_conv(cg_c * xin_c, conv_w)
    out_ctx = jnp.concatenate([na_c, dif_c, conv_c], axis=-1) @ w_out
    return out_lat, out_ctx


def setup_inputs(seed: int = 0) -> dict:
    key = jax.random.key(seed)
    ks = jax.random.split(key, 20)
    f32 = jnp.float32

    def nrm(k, shape, scale):
        return jax.random.normal(k, shape, f32) * scale

    return {
        'x': nrm(ks[0], (BATCH, SEQ, D_MODEL), 1.0),
        'c': nrm(ks[1], (BATCH, D_MODEL), 1.0),
        'ctx': nrm(ks[2], (BATCH, CTX_LEN, D_MODEL), 1.0),
        'c_ctx': nrm(ks[3], (D_MODEL,), 1.0),
        'ada_w': nrm(ks[4], (DEPTH, D_MODEL, 6 * D_MODEL), 0.5 * D_MODEL ** -0.5),
        'ada_b': nrm(ks[5], (DEPTH, 6 * D_MODEL), 0.01),
        'w_in': nrm(ks[6], (DEPTH, D_MODEL, IN_COLS), D_MODEL ** -0.5),
        'w_out': nrm(ks[7], (DEPTH, MIX_WIDTH, D_MODEL), MIX_WIDTH ** -0.5),
        'na_rpb': nrm(ks[8], (DEPTH, NA_HEADS, 2 * WIN_H - 1, 2 * WIN_W - 1), 0.1),
        'diff_lambda': nrm(ks[9], (DEPTH, 4, HEAD_DIM), 0.1),
        'diff_subln': 1.0 + nrm(ks[10], (DEPTH, 2 * HEAD_DIM), 0.02),
        'conv_w': nrm(ks[11], (DEPTH, CONV_K, CONV_WIDTH), CONV_K ** -0.5),
        'ffn_w1': nrm(ks[12], (N_DENSE, D_MODEL, D_FF), D_MODEL ** -0.5),
        'ffn_w3': nrm(ks[13], (N_DENSE, D_MODEL, D_FF), D_MODEL ** -0.5),
        'ffn_w2': nrm(ks[14], (N_DENSE, D_FF, D_MODEL), D_FF ** -0.5),
        'router_w': nrm(ks[15], (N_MOE, D_MODEL, N_EXPERTS), D_MODEL ** -0.5),
        'moe_w1': nrm(ks[16], (N_MOE, N_EXPERTS, D_MODEL, D_EXPERT), D_MODEL ** -0.5),
        'moe_w3': nrm(ks[17], (N_MOE, N_EXPERTS, D_MODEL, D_EXPERT), D_MODEL ** -0.5),
        'moe_w2': nrm(ks[18], (N_MOE, N_EXPERTS, D_EXPERT, D_MODEL), D_EXPERT ** -0.5),
        'final_gain': 1.0 + nrm(ks[19], (D_MODEL,), 0.02),
    }


def reference(x, c, ctx, c_ctx, ada_w, ada_b, w_in, w_out, na_rpb, diff_lambda, diff_subln, conv_w,
              ffn_w1, ffn_w3, ffn_w2, router_w, moe_w1, moe_w3, moe_w2, final_gain):
    n = x.shape[1]
    cos, sin = axial_rope(n, x.dtype)
    x_lat, x_ctx = x, ctx
    for i in range(DEPTH):
        ctx_out = i < DEPTH - 1
        mod_lat = (jax.nn.silu(c) @ ada_w[i] + ada_b[i])[:, None, :]
        mod_ctx = jax.nn.silu(c_ctx) @ ada_w[i] + ada_b[i]
        sh_a, sc_a, g_a, sh_f, sc_f, g_f = jnp.split(mod_lat, 6, axis=-1)
        csh_a, csc_a, cg_a, csh_f, csc_f, cg_f = jnp.split(mod_ctx, 6, axis=-1)
        lq1, lk1, lq2, lk2 = [t.astype(jnp.float32) for t in diff_lambda[i]]
        lambda_init = 0.8 - 0.6 * math.exp(-0.3 * i)
        lam = jnp.exp(jnp.sum(lq1 * lk1)) - jnp.exp(jnp.sum(lq2 * lk2)) + lambda_init
        h_lat = modulate(rms_norm(x_lat), sh_a, sc_a)
        h_ctx = modulate(rms_norm(x_ctx), csh_a, csc_a)
        out_lat, out_ctx = token_mixer(h_lat, h_ctx, w_in[i], w_out[i], na_rpb[i], lam, lambda_init,
                                       diff_subln[i], conv_w[i], cos, sin, ctx_out)
        x_lat = x_lat + g_a * out_lat
        if i % 2 == 0:
            m = i // 2
            ffn = functools_partial_swiglu(ffn_w1[m], ffn_w3[m], ffn_w2[m])
        else:
            m = i // 2
            ffn = functools_partial_moe(router_w[m], moe_w1[m], moe_w3[m], moe_w2[m])
        x_lat = x_lat + g_f * ffn(modulate(rms_norm(x_lat), sh_f, sc_f))
        if ctx_out:
            x_ctx = x_ctx + cg_a * out_ctx
            x_ctx = x_ctx + cg_f * ffn(modulate(rms_norm(x_ctx), csh_f, csc_f))
    return rms_norm(x_lat, final_gain)


def functools_partial_swiglu(w1, w3, w2):
    return lambda h: swiglu(h, w1, w3, w2)


def functools_partial_moe(w_router, w1, w3, w2):
    return lambda h: moe_ffn(h, w_router, w1, w3, w2)
```

```python
import functools
import math

import numpy as np
import jax
import jax.numpy as jnp
from jax import lax
from jax.experimental import pallas as pl
from jax.experimental.pallas import tpu as pltpu

D_MODEL = 1024
GRID_W = 64
HEAD_DIM = 64
NA_HEADS = 4
NA_WIDTH = NA_HEADS * HEAD_DIM
WIN_H = 8
WIN_W = 16
DIFF_HEADS = 4
DIFF_WIDTH = DIFF_HEADS * 2 * HEAD_DIM
CONV_WIDTH = 256
QKV_COLS = NA_WIDTH + DIFF_WIDTH
IN_COLS = 3 * QKV_COLS + 3 * CONV_WIDTH
ROPE_BASE = 10000.0
N_EXPERTS = 8
EPS = 1e-6
NEG_BIG = -1e30

LANES = 128
TOKEN_TILE = 256
NA_ROWS = 4
NA_KEY_ROWS = NA_ROWS + WIN_H
VMEM_LIMIT = 56 * 1024 * 1024


def _cparams(sem):
    return pltpu.CompilerParams(dimension_semantics=sem, vmem_limit_bytes=VMEM_LIMIT)


def _rms(x):
    return x * lax.rsqrt(jnp.mean(x * x, axis=-1, keepdims=True) + EPS)


def _mod_kernel(c_ref, w_ref, b_ref, o_ref):
    cv = c_ref[...]
    s = cv * jax.nn.sigmoid(cv)
    o_ref[...] = jnp.dot(s, w_ref[...], preferred_element_type=jnp.float32,
                         precision=lax.Precision.HIGHEST) + b_ref[...]


def _modulation(cvec, ada_w, ada_b):
    depth, d, cols = ada_w.shape
    cb = 1536
    return pl.pallas_call(
        _mod_kernel,
        grid=(depth, cols // cb),
        in_specs=[pl.BlockSpec((8, d), lambda i, j: (0, 0)),
                  pl.BlockSpec((None, d, cb), lambda i, j: (i, 0, j)),
                  pl.BlockSpec((None, 1, cb), lambda i, j: (i, 0, j))],
        out_specs=pl.BlockSpec((None, 8, cb), lambda i, j: (i, 0, j)),
        out_shape=jax.ShapeDtypeStruct((depth, 8, cols), jnp.float32),
        compiler_params=_cparams(("parallel", "parallel")),
        name="modulation",
    )(cvec, ada_w, ada_b.reshape(depth, 1, cols))


def _rope(z, cos, sin_signed):
    width = z.shape[1]
    reps = width // LANES
    c = jnp.concatenate([cos] * reps, axis=1)
    s = jnp.concatenate([sin_signed] * reps, axis=1)
    lane = lax.broadcasted_iota(jnp.int32, (1, width), 1)
    first_half = (lane % HEAD_DIM) < (HEAD_DIM // 2)
    swapped = jnp.where(first_half,
                        pltpu.roll(z, width - HEAD_DIM // 2, axis=1),
                        pltpu.roll(z, HEAD_DIM // 2, axis=1))
    return z * c + swapped * s


def _in_proj_kernel(x_ref, mod_ref, cos_ref, sin_ref, w_ref, q_ref, k_ref, v_ref, u_ref, bg_ref):
    x = x_ref[...]
    h = _rms(x) * (1.0 + mod_ref[1:2, :]) + mod_ref[0:1, :]
    p = jnp.dot(h.astype(jnp.bfloat16), w_ref[...], preferred_element_type=jnp.float32)
    cos = cos_ref[...]
    sin = sin_ref[...]
    scale = HEAD_DIM ** -0.5
    q_ref[:, :NA_WIDTH] = (p[:, :NA_WIDTH] * scale).astype(q_ref.dtype)
    q_ref[:, NA_WIDTH:] = (_rope(p[:, NA_WIDTH:QKV_COLS], cos, sin) * scale).astype(q_ref.dtype)
    k_ref[:, :NA_WIDTH] = p[:, QKV_COLS:QKV_COLS + NA_WIDTH].astype(k_ref.dtype)
    k_ref[:, NA_WIDTH:] = _rope(p[:, QKV_COLS + NA_WIDTH:2 * QKV_COLS], cos, sin).astype(k_ref.dtype)
    v_ref[...] = p[:, 2 * QKV_COLS:3 * QKV_COLS].astype(v_ref.dtype)
    o = 3 * QKV_COLS
    xin = p[:, o:o + CONV_WIDTH]
    bg_ref[...] = p[:, o + CONV_WIDTH:o + 2 * CONV_WIDTH]
    u_ref[...] = p[:, o + 2 * CONV_WIDTH:o + 3 * CONV_WIDTH] * xin


def _in_proj(x_all, modtab, cos_t, sin_t, w_bf16, n_lat):
    b, t, d = x_all.shape
    tm = TOKEN_TILE
    nlt = n_lat // tm
    tok = lambda width: pl.BlockSpec((None, tm, width), lambda bi, i: (bi, i, 0))
    return pl.pallas_call(
        _in_proj_kernel,
        grid=(b, t // tm),
        in_specs=[tok(d),
                  pl.BlockSpec((None, 6, d), lambda bi, i: (bi * 2 + (i >= nlt).astype(jnp.int32), 0, 0)),
                  pl.BlockSpec((tm, LANES), lambda bi, i: (i, 0)),
                  pl.BlockSpec((tm, LANES), lambda bi, i: (i, 0)),
                  pl.BlockSpec((d, IN_COLS), lambda bi, i: (0, 0))],
        out_specs=[tok(QKV_COLS), tok(QKV_COLS), tok(QKV_COLS), tok(CONV_WIDTH), tok(CONV_WIDTH)],
        out_shape=[jax.ShapeDtypeStruct((b, t, QKV_COLS), jnp.bfloat16)] * 3
        + [jax.ShapeDtypeStruct((b, t, CONV_WIDTH), jnp.float32)] * 2,
        compiler_params=_cparams(("parallel", "parallel")),
        name="in_proj",
    )(x_all, modtab, cos_t, sin_t, w_bf16)


def _na_kernel(*refs, n_lat_blocks):
    q_ref = refs[0]
    k_refs = refs[1:2 + n_lat_blocks]
    v_refs = refs[2 + n_lat_blocks:3 + 2 * n_lat_blocks]
    if n_lat_blocks:
        bias_ref, o_ref = refs[3 + 2 * n_lat_blocks:]
    else:
        (o_ref,) = refs[3 + 2 * n_lat_blocks:]
    lane = lax.broadcasted_iota(jnp.int32, (1, LANES), 1)
    lat_keys = n_lat_blocks * TOKEN_TILE
    for pair in range(NA_HEADS // 2):
        cols = slice(pair * LANES, (pair + 1) * LANES)
        q = q_ref[:, cols]
        outs = []
        for hh in range(2):
            head = 2 * pair + hh
            in_head = (lane >= hh * HEAD_DIM) & (lane < (hh + 1) * HEAD_DIM)
            qm = jnp.where(in_head, q, jnp.zeros_like(q))
            s = jnp.concatenate(
                [lax.dot_general(qm, kr[:, cols], (((1,), (1,)), ((), ())),
                                 preferred_element_type=jnp.float32) for kr in k_refs], axis=1)
            if n_lat_blocks:
                s = jnp.concatenate([s[:, :lat_keys] + bias_ref[head], s[:, lat_keys:]], axis=1)
            m = jnp.max(s, axis=-1, keepdims=True)
            e = jnp.exp(s - m)
            l = jnp.sum(e, axis=-1, keepdims=True)
            eb = e.astype(jnp.bfloat16)
            acc = None
            for j, vr in enumerate(v_refs):
                part = jnp.dot(eb[:, j * TOKEN_TILE:(j + 1) * TOKEN_TILE], vr[:, cols],
                               preferred_element_type=jnp.float32)
                acc = part if acc is None else acc + part
            outs.append(acc / l)
        o_ref[:, cols] = jnp.where(lane < HEAD_DIM, outs[0], outs[1]).astype(o_ref.dtype)


def _na_bias_table(rpb, rows):
    n_groups = rows // NA_ROWS
    tables = []
    for g in (0, 1, n_groups - 1):
        r0 = g * NA_ROWS
        ws = min(max(g - 1, 0), n_groups - 3) * NA_ROWS
        qr = r0 + np.arange(NA_ROWS)[:, None, None, None]
        qc = np.arange(GRID_W)[None, :, None, None]
        kr = ws + np.arange(NA_KEY_ROWS)[None, None, :, None]
        kc = np.arange(GRID_W)[None, None, None, :]
        row_start = np.clip(qr - WIN_H // 2, 0, rows - WIN_H)
        col_start = np.clip(qc - WIN_W // 2, 0, GRID_W - WIN_W)
        valid = (kr >= row_start) & (kr < row_start + WIN_H) & (kc >= col_start) & (kc < col_start + WIN_W)
        drow = np.clip(kr - qr + WIN_H - 1, 0, 2 * WIN_H - 2)
        dcol = np.clip(kc - qc + WIN_W - 1, 0, 2 * WIN_W - 2)
        shape = (NA_ROWS, GRID_W, NA_KEY_ROWS, GRID_W)
        drow = np.broadcast_to(drow, shape).reshape(NA_ROWS * GRID_W, NA_KEY_ROWS * GRID_W)
        dcol = np.broadcast_to(dcol, shape).reshape(NA_ROWS * GRID_W, NA_KEY_ROWS * GRID_W)
        valid = np.broadcast_to(valid, shape).reshape(NA_ROWS * GRID_W, NA_KEY_ROWS * GRID_W)
        bias = rpb.astype(jnp.float32)[:, drow, dcol]
        tables.append(jnp.where(valid[None], bias, NEG_BIG))
    return jnp.stack(tables)


def _na_latent(q, k, v, bias, n_lat):
    b = q.shape[0]
    tm = TOKEN_TILE
    assert NA_ROWS * GRID_W == tm and NA_KEY_ROWS * GRID_W == 3 * tm
    ng = n_lat // tm
    ctx_blk = n_lat // tm

    def kv_spec(j):
        return pl.BlockSpec((None, tm, NA_WIDTH), lambda bi, g: (bi, jnp.clip(g - 1, 0, ng - 3) + j, 0))

    ctx_spec = pl.BlockSpec((None, tm, NA_WIDTH), lambda bi, g: (bi, ctx_blk, 0))
    variant = lambda bi, g: ((g > 0).astype(jnp.int32) + (g == ng - 1).astype(jnp.int32), 0, 0, 0)
    return pl.pallas_call(
        functools.partial(_na_kernel, n_lat_blocks=3),
        grid=(b, ng),
        in_specs=[pl.BlockSpec((None, tm, NA_WIDTH), lambda bi, g: (bi, g, 0)),
                  kv_spec(0), kv_spec(1), kv_spec(2), ctx_spec,
                  kv_spec(0), kv_spec(1), kv_spec(2), ctx_spec,
                  pl.BlockSpec((None, NA_HEADS, tm, 3 * tm), variant)],
        out_specs=pl.BlockSpec((None, tm, NA_WIDTH), lambda bi, g: (bi, g, 0)),
        out_shape=jax.ShapeDtypeStruct((b, n_lat, NA_WIDTH), jnp.bfloat16),
        compiler_params=_cparams(("parallel", "parallel")),
        name="na_latent",
    )(q, k, k, k, k, v, v, v, v, bias)


def _na_context(q, k, v, n_lat):
    b = q.shape[0]
    tm = TOKEN_TILE
    ctx_spec = pl.BlockSpec((None, tm, NA_WIDTH), lambda bi: (bi, n_lat // tm, 0))
    return pl.pallas_call(
        functools.partial(_na_kernel, n_lat_blocks=0),
        grid=(b,),
        in_specs=[ctx_spec, ctx_spec, ctx_spec],
        out_specs=pl.BlockSpec((None, tm, NA_WIDTH), lambda bi: (bi, 0, 0)),
        out_shape=jax.ShapeDtypeStruct((b, tm, NA_WIDTH), jnp.bfloat16),
        compiler_params=_cparams(("parallel",)),
        name="na_context",
    )(q, k, v)


def _diff_kernel(q_ref, k_ref, v_ref, lam_ref, gain_ref, o_ref, m_ref, l_ref, acc_ref, *, lambda_init):
    kk = pl.program_id(3)

    @pl.when(kk == 0)
    def _():
        m_ref[...] = jnp.full(m_ref.shape, NEG_BIG, jnp.float32)
        l_ref[...] = jnp.zeros(l_ref.shape, jnp.float32)
        acc_ref[...] = jnp.zeros(acc_ref.shape, jnp.float32)

    lane = lax.broadcasted_iota(jnp.int32, (1, LANES), 1)
    q = q_ref[...]
    k = k_ref[...]
    v = v_ref[...]
    for hh in range(2):
        in_head = (lane >= hh * HEAD_DIM) & (lane < (hh + 1) * HEAD_DIM)
        qm = jnp.where(in_head, q, jnp.zeros_like(q))
        s = lax.dot_general(qm, k, (((1,), (1,)), ((), ())), preferred_element_type=jnp.float32)
        m_old = m_ref[hh]
        m_new = jnp.maximum(m_old, jnp.max(s, axis=-1, keepdims=True))
        alpha = jnp.exp(m_old - m_new)
        e = jnp.exp(s - m_new)
        l_ref[hh] = alpha * l_ref[hh] + jnp.sum(e, axis=-1, keepdims=True)
        acc_ref[hh] = alpha * acc_ref[hh] + jnp.dot(e.astype(jnp.bfloat16), v,
                                                    preferred_element_type=jnp.float32)
        m_ref[hh] = m_new

    @pl.when(kk == pl.num_programs(3) - 1)
    def _():
        lp = lam_ref[...]
        lam = (jnp.exp(jnp.sum(lp[0:1] * lp[1:2], axis=-1, keepdims=True))
               - jnp.exp(jnp.sum(lp[2:3] * lp[3:4], axis=-1, keepdims=True)) + lambda_init)
        o = acc_ref[0] / l_ref[0] - lam * (acc_ref[1] / l_ref[1])
        o_ref[...] = (_rms(o) * gain_ref[...] * (1.0 - lambda_init)).astype(o_ref.dtype)


def _diff_attention(q, k, v, lam_params, gain, lambda_init, *, q_rows, q_start, tq, k_rows, k_start, tk):
    b = q.shape[0]
    nq, nk = q_rows // tq, k_rows // tk
    qo, ko = q_start // tq, k_start // tk
    first = NA_WIDTH // LANES
    return pl.pallas_call(
        functools.partial(_diff_kernel, lambda_init=lambda_init),
        grid=(b, DIFF_HEADS, nq, nk),
        in_specs=[pl.BlockSpec((None, tq, LANES), lambda bi, h, i, j: (bi, qo + i, first + h)),
                  pl.BlockSpec((None, tk, LANES), lambda bi, h, i, j: (bi, ko + j, first + h)),
                  pl.BlockSpec((None, tk, LANES), lambda bi, h, i, j: (bi, ko + j, first + h)),
                  pl.BlockSpec((4, HEAD_DIM), lambda bi, h, i, j: (0, 0)),
                  pl.BlockSpec((1, LANES), lambda bi, h, i, j: (0, 0))],
        out_specs=pl.BlockSpec((None, tq, LANES), lambda bi, h, i, j: (bi, i, h)),
        out_shape=jax.ShapeDtypeStruct((b, q_rows, DIFF_WIDTH), jnp.bfloat16),
        scratch_shapes=[pltpu.VMEM((2, tq, 1), jnp.float32),
                        pltpu.VMEM((2, tq, 1), jnp.float32),
                        pltpu.VMEM((2, tq, LANES), jnp.float32)],
        compiler_params=_cparams(("parallel", "parallel", "parallel", "arbitrary")),
        name="diff_attention",
    )(q, k, v, lam_params, gain)


def _out_proj_kernel(x_ref, na_ref, dif_ref, u_ref, up_ref, un_ref, bg_ref, cw_ref, w_ref, mod_ref,
                     *rest, n_lat_tiles, router):
    if router:
        rw_ref, x1_ref, h2_ref, gates_ref = rest
    else:
        x1_ref, h2_ref = rest
    i = pl.program_id(1)
    tm = x_ref.shape[0]
    first_of_seq = (i == 0) | (i == n_lat_tiles)
    last_of_seq = (i == n_lat_tiles - 1) | (i == n_lat_tiles)
    u = u_ref[...]
    up = jnp.where(first_of_seq, 0.0, up_ref[7:8, :])
    un = jnp.where(last_of_seq, 0.0, un_ref[0:1, :])
    row = lax.broadcasted_iota(jnp.int32, (tm, 1), 0)
    u_prev = jnp.where(row == 0, up, pltpu.roll(u, 1, axis=0))
    u_next = jnp.where(row == tm - 1, un, pltpu.roll(u, tm - 1, axis=0))
    conv = bg_ref[...] * (cw_ref[0:1, :] * u_prev + cw_ref[1:2, :] * u + cw_ref[2:3, :] * u_next)
    o1 = NA_WIDTH
    o2 = NA_WIDTH + DIFF_WIDTH
    mix = (jnp.dot(na_ref[...], w_ref[:o1, :], preferred_element_type=jnp.float32)
           + jnp.dot(dif_ref[...], w_ref[o1:o2, :], preferred_element_type=jnp.float32)
           + jnp.dot(conv.astype(jnp.bfloat16), w_ref[o2:, :], preferred_element_type=jnp.float32))
    x1 = x_ref[...] + mod_ref[2:3, :] * mix
    x1_ref[...] = x1
    h2 = _rms(x1) * (1.0 + mod_ref[4:5, :]) + mod_ref[3:4, :]
    h2_ref[...] = h2.astype(h2_ref.dtype)
    if router:
        logits = jnp.dot(h2, rw_ref[...], preferred_element_type=jnp.float32,
                         precision=lax.Precision.HIGHEST)
        lane = lax.broadcasted_iota(jnp.int32, logits.shape, 1)
        logits = jnp.where(lane < N_EXPERTS, logits, NEG_BIG)
        m1 = jnp.max(logits, axis=-1, keepdims=True)
        i1 = jnp.min(jnp.where(logits == m1, lane, LANES), axis=-1, keepdims=True)
        rest_l = jnp.where(lane == i1, NEG_BIG, logits)
        m2 = jnp.max(rest_l, axis=-1, keepdims=True)
        i2 = jnp.min(jnp.where(rest_l == m2, lane, LANES), axis=-1, keepdims=True)
        e2 = jnp.exp(m2 - m1)
        w1 = 1.0 / (1.0 + e2)
        gates_ref[...] = jnp.where(lane == i1, w1, jnp.where(lane == i2, e2 * w1, 0.0))


def _out_proj(x_all, na, dif, u, bg, conv_w, w_bf16, modtab, n_lat, n_rows, router_w=None):
    b, t, d = x_all.shape
    tm = TOKEN_TILE
    nlt = n_lat // tm
    sub = 8
    n_sub = t // sub
    tok = lambda width: pl.BlockSpec((None, tm, width), lambda bi, i: (bi, i, 0))
    in_specs = [tok(d), tok(NA_WIDTH), tok(DIFF_WIDTH), tok(CONV_WIDTH),
                pl.BlockSpec((None, sub, CONV_WIDTH),
                             lambda bi, i: (bi, jnp.maximum(i * (tm // sub) - 1, 0), 0)),
                pl.BlockSpec((None, sub, CONV_WIDTH),
                             lambda bi, i: (bi, jnp.minimum((i + 1) * (tm // sub), n_sub - 1), 0)),
                tok(CONV_WIDTH),
                pl.BlockSpec((3, CONV_WIDTH), lambda bi, i: (0, 0)),
                pl.BlockSpec((d, d), lambda bi, i: (0, 0)),
                pl.BlockSpec((None, 6, d), lambda bi, i: (bi * 2 + (i >= nlt).astype(jnp.int32), 0, 0))]
    out_specs = [tok(d), tok(d)]
    out_shape = [jax.ShapeDtypeStruct((b, n_rows, d), jnp.float32),
                 jax.ShapeDtypeStruct((b, n_rows, d), jnp.bfloat16)]
    args = [x_all, na, dif, u, u, u, bg, conv_w, w_bf16, modtab]
    if router_w is not None:
        in_specs.append(pl.BlockSpec((d, LANES), lambda bi, i: (0, 0)))
        out_specs.append(tok(LANES))
        out_shape.append(jax.ShapeDtypeStruct((b, n_rows, LANES), jnp.float32))
        args.append(router_w)
    return pl.pallas_call(
        functools.partial(_out_proj_kernel, n_lat_tiles=nlt, router=router_w is not None),
        grid=(b, n_rows // tm),
        in_specs=in_specs, out_specs=out_specs, out_shape=out_shape,
        compiler_params=_cparams(("parallel", "parallel")),
        name="out_proj",
    )(*args)


def _ffn_kernel(x1_ref, h_ref, w1_ref, w3_ref, w2_ref, mod_ref, o_ref):
    h = h_ref[...]
    a = jnp.dot(h, w1_ref[...], preferred_element_type=jnp.float32)
    g = jnp.dot(h, w3_ref[...], preferred_element_type=jnp.float32)
    act = (a * jax.nn.sigmoid(a) * g).astype(jnp.bfloat16)
    y = jnp.dot(act, w2_ref[...], preferred_element_type=jnp.float32)
    o_ref[...] = x1_ref[...] + mod_ref[5:6, :] * y


def _dense_ffn(x1, h2, w1, w3, w2, modtab, n_lat):
    b, t, d = x1.shape
    tm = TOKEN_TILE
    nlt = n_lat // tm
    ff = w1.shape[1]
    tok = pl.BlockSpec((None, tm, d), lambda bi, i: (bi, i, 0))
    const = lambda shape: pl.BlockSpec(shape, lambda bi, i: (0, 0), pipeline_mode=pl.Buffered(1))
    return pl.pallas_call(
        _ffn_kernel,
        grid=(b, t // tm),
        in_specs=[tok, tok, const((d, ff)), const((d, ff)), const((ff, d)),
                  pl.BlockSpec((None, 6, d), lambda bi, i: (bi * 2 + (i >= nlt).astype(jnp.int32), 0, 0))],
        out_specs=tok,
        out_shape=jax.ShapeDtypeStruct((b, t, d), jnp.float32),
        compiler_params=_cparams(("parallel", "parallel")),
        name="dense_ffn",
    )(x1, h2, w1, w3, w2, modtab)


def _moe_kernel(x1_ref, h_ref, gates_ref, w1_ref, w3_ref, w2_ref, mod_ref, gain_ref, o_ref, acc_ref):
    e = pl.program_id(2)
    c = pl.program_id(3)

    @pl.when((e == 0) & (c == 0))
    def _():
        acc_ref[...] = jnp.zeros(acc_ref.shape, jnp.float32)

    lane = lax.broadcasted_iota(jnp.int32, gates_ref.shape, 1)
    gate = jnp.sum(jnp.where(lane == e, gates_ref[...], 0.0), axis=-1, keepdims=True)
    h = h_ref[...]
    a = jnp.dot(h, w1_ref[...], preferred_element_type=jnp.float32)
    g = jnp.dot(h, w3_ref[...], preferred_element_type=jnp.float32)
    act = (a * jax.nn.sigmoid(a) * g * gate).astype(jnp.bfloat16)
    acc_ref[...] += jnp.dot(act, w2_ref[...], preferred_element_type=jnp.float32)

    @pl.when((e == pl.num_programs(2) - 1) & (c == pl.num_programs(3) - 1))
    def _():
        x2 = x1_ref[...] + mod_ref[5:6, :] * acc_ref[...]
        o_ref[...] = _rms(x2) * gain_ref[...]


def _moe_ffn_final(x1, h2, gates, w1, w3, w2, modtab, final_gain):
    b, n, d = x1.shape
    tm = 512
    n_exp, _, ff = w1.shape
    fc = ff // 2
    tok = lambda width: pl.BlockSpec((None, tm, width), lambda bi, i, e, c: (bi, i, 0))
    return pl.pallas_call(
        _moe_kernel,
        grid=(b, n // tm, n_exp, ff // fc),
        in_specs=[tok(d), tok(d), tok(LANES),
                  pl.BlockSpec((None, d, fc), lambda bi, i, e, c: (e, 0, c)),
                  pl.BlockSpec((None, d, fc), lambda bi, i, e, c: (e, 0, c)),
                  pl.BlockSpec((None, fc, d), lambda bi, i, e, c: (e, c, 0)),
                  pl.BlockSpec((None, 6, d), lambda bi, i, e, c: (bi * 2, 0, 0)),
                  pl.BlockSpec((1, d), lambda bi, i, e, c: (0, 0))],
        out_specs=tok(d),
        out_shape=jax.ShapeDtypeStruct((b, n, d), jnp.float32),
        scratch_shapes=[pltpu.VMEM((tm, d), jnp.float32)],
        compiler_params=_cparams(("parallel", "parallel", "arbitrary", "arbitrary")),
        name="moe_ffn",
    )(x1, h2, gates, w1, w3, w2, modtab, final_gain.reshape(1, d))


def _rope_tables(n_lat, n_ctx):
    t = jnp.arange(n_lat, dtype=jnp.int32)
    n_freq = HEAD_DIM // 4
    inv_freq = ROPE_BASE ** (-jnp.arange(n_freq, dtype=jnp.float32) / n_freq)
    ang = jnp.concatenate([(t // GRID_W).astype(jnp.float32)[:, None] * inv_freq,
                           (t % GRID_W).astype(jnp.float32)[:, None] * inv_freq], axis=-1)
    cos, sin = jnp.cos(ang), jnp.sin(ang)
    reps = LANES // HEAD_DIM
    cos_t = jnp.tile(jnp.concatenate([cos, cos], axis=-1), (1, reps))
    sin_t = jnp.tile(jnp.concatenate([-sin, sin], axis=-1), (1, reps))
    cos_t = jnp.concatenate([cos_t, jnp.ones((n_ctx, LANES), jnp.float32)], axis=0)
    sin_t = jnp.concatenate([sin_t, jnp.zeros((n_ctx, LANES), jnp.float32)], axis=0)
    return cos_t, sin_t


def kernel(x, c, ctx, c_ctx, ada_w, ada_b, w_in, w_out, na_rpb, diff_lambda, diff_subln, conv_w,
           ffn_w1, ffn_w3, ffn_w2, router_w, moe_w1, moe_w3, moe_w2, final_gain):
    b, n, d = x.shape
    n_ctx = ctx.shape[1]
    depth = w_in.shape[0]
    assert d == D_MODEL and n % TOKEN_TILE == 0 and n_ctx == TOKEN_TILE and b + 1 <= 8
    assert depth == 2, "layer 0 dense with a context stream, layer 1 routed and final"
    bf = jnp.bfloat16

    x_all = jnp.concatenate([x, ctx], axis=1)
    cvec = jnp.zeros((8, d), jnp.float32).at[:b].set(c).at[b].set(c_ctx)
    mod = _modulation(cvec, ada_w, ada_b).reshape(depth, 8, 6, d)
    cos_t, sin_t = _rope_tables(n, n_ctx)
    rows = n // GRID_W

    out = None
    for i in range(depth):
        lambda_init = 0.8 - 0.6 * math.exp(-0.3 * i)
        ctx_out = i < depth - 1
        modtab = jnp.stack([mod[i, :b], jnp.broadcast_to(mod[i, b], (b, 6, d))], axis=1).reshape(2 * b, 6, d)
        q, k, v, u, bg = _in_proj(x_all, modtab, cos_t, sin_t, w_in[i].astype(bf), n)
        bias = _na_bias_table(na_rpb[i], rows)
        na = _na_latent(q, k, v, bias, n)
        gain = diff_subln[i].reshape(1, 2 * HEAD_DIM)
        dif = _diff_attention(q, k, v, diff_lambda[i], gain, lambda_init,
                              q_rows=n, q_start=0, tq=512, k_rows=n + n_ctx, k_start=0, tk=768)
        if ctx_out:
            na_c = _na_context(q, k, v, n)
            dif_c = _diff_attention(q, k, v, diff_lambda[i], gain, lambda_init,
                                    q_rows=n_ctx, q_start=n, tq=n_ctx, k_rows=n_ctx, k_start=n, tk=n_ctx)
            na = jnp.concatenate([na, na_c], axis=1)
            dif = jnp.concatenate([dif, dif_c], axis=1)
            x1, h2 = _out_proj(x_all, na, dif, u, bg, conv_w[i], w_out[i].astype(bf), modtab, n, n + n_ctx)
            m = i // 2
            x_all = _dense_ffn(x1, h2, ffn_w1[m].astype(bf), ffn_w3[m].astype(bf), ffn_w2[m].astype(bf),
                               modtab, n)
        else:
            m = i // 2
            rw = jnp.zeros((d, LANES), jnp.float32).at[:, :N_EXPERTS].set(router_w[m])
            x1, h2, gates = _out_proj(x_all, na, dif, u, bg, conv_w[i], w_out[i].astype(bf), modtab, n, n,
                                      router_w=rw)
            out = _moe_ffn_final(x1, h2, gates, moe_w1[m].astype(bf), moe_w3[m].astype(bf),
                                 moe_w2[m].astype(bf), modtab, final_gain)
    return out
```

```python
import functools
import math

import numpy as np
import jax
import jax.numpy as jnp
from jax import lax
from jax.experimental import pallas as pl
from jax.experimental.pallas import tpu as pltpu

D_MODEL = 1024
GRID_W = 64
HEAD_DIM = 64
NA_HEADS = 4
NA_WIDTH = NA_HEADS * HEAD_DIM
WIN_H = 8
WIN_W = 16
DIFF_HEADS = 4
DIFF_WIDTH = DIFF_HEADS * 2 * HEAD_DIM
CONV_WIDTH = 256
QKV_COLS = NA_WIDTH + DIFF_WIDTH
NA_COL_BLOCK = DIFF_WIDTH // NA_WIDTH
IN_COLS = 3 * QKV_COLS + 3 * CONV_WIDTH
ROPE_BASE = 10000.0
N_EXPERTS = 8
EPS = 1e-6
NEG_BIG = -1e30
LOG2E = math.log2(math.e)

LANES = 128
TOKEN_TILE = 256
NA_ROWS = 4
NA_KEY_ROWS = NA_ROWS + WIN_H
VMEM_LIMIT = 56 * 1024 * 1024


def _cparams(sem):
    return pltpu.CompilerParams(dimension_semantics=sem, vmem_limit_bytes=VMEM_LIMIT)


def _rms(x):
    return x * lax.rsqrt(jnp.mean(x * x, axis=-1, keepdims=True) + EPS)


def _mod_kernel(c_ref, w_ref, b_ref, o_ref):
    cv = c_ref[...]
    s = cv * jax.nn.sigmoid(cv)
    o_ref[...] = jnp.dot(s, w_ref[...], preferred_element_type=jnp.float32,
                         precision=lax.Precision.HIGHEST) + b_ref[...]


def _modulation(cvec, ada_w, ada_b):
    depth, d, cols = ada_w.shape
    cb = 1536
    return pl.pallas_call(
        _mod_kernel,
        grid=(depth, cols // cb),
        in_specs=[pl.BlockSpec((8, d), lambda i, j: (0, 0)),
                  pl.BlockSpec((None, d, cb), lambda i, j: (i, 0, j)),
                  pl.BlockSpec((None, 1, cb), lambda i, j: (i, 0, j))],
        out_specs=pl.BlockSpec((None, 8, cb), lambda i, j: (i, 0, j)),
        out_shape=jax.ShapeDtypeStruct((depth, 8, cols), jnp.float32),
        compiler_params=_cparams(("parallel", "parallel")),
        name="modulation",
    )(cvec, ada_w, ada_b.reshape(depth, 1, cols))


def _rope(z, cos, sin_signed):
    width = z.shape[1]
    reps = width // LANES
    c = jnp.concatenate([cos] * reps, axis=1)
    s = jnp.concatenate([sin_signed] * reps, axis=1)
    lane = lax.broadcasted_iota(jnp.int32, (1, width), 1)
    first_half = (lane % HEAD_DIM) < (HEAD_DIM // 2)
    swapped = jnp.where(first_half,
                        pltpu.roll(z, width - HEAD_DIM // 2, axis=1),
                        pltpu.roll(z, HEAD_DIM // 2, axis=1))
    return z * c + swapped * s


def _in_proj_kernel(x_ref, mod_ref, cos_ref, sin_ref, w_ref, q_ref, k_ref, v_ref, u_ref, bg_ref):
    x = x_ref[...]
    h = _rms(x) * (1.0 + mod_ref[1:2, :]) + mod_ref[0:1, :]
    p = jnp.dot(h.astype(jnp.bfloat16), w_ref[...], preferred_element_type=jnp.float32)
    cos = cos_ref[...]
    sin = sin_ref[...]
    scale = HEAD_DIM ** -0.5 * LOG2E
    q_ref[:, :DIFF_WIDTH] = (_rope(p[:, NA_WIDTH:QKV_COLS], cos, sin) * scale).astype(q_ref.dtype)
    q_ref[:, DIFF_WIDTH:] = (p[:, :NA_WIDTH] * scale).astype(q_ref.dtype)
    k_ref[:, :DIFF_WIDTH] = _rope(p[:, QKV_COLS + NA_WIDTH:2 * QKV_COLS], cos, sin).astype(k_ref.dtype)
    k_ref[:, DIFF_WIDTH:] = p[:, QKV_COLS:QKV_COLS + NA_WIDTH].astype(k_ref.dtype)
    o = 2 * QKV_COLS
    v_ref[:, :DIFF_WIDTH] = p[:, o + NA_WIDTH:o + QKV_COLS].astype(v_ref.dtype)
    v_ref[:, DIFF_WIDTH:] = p[:, o:o + NA_WIDTH].astype(v_ref.dtype)
    o = 3 * QKV_COLS
    xin = p[:, o:o + CONV_WIDTH]
    bg_ref[...] = p[:, o + CONV_WIDTH:o + 2 * CONV_WIDTH]
    u_ref[...] = p[:, o + 2 * CONV_WIDTH:o + 3 * CONV_WIDTH] * xin


def _in_proj(x_all, modtab, cos_t, sin_t, w_bf16, n_lat):
    b, t, d = x_all.shape
    tm = TOKEN_TILE
    nlt = n_lat // tm
    tok = lambda width: pl.BlockSpec((None, tm, width), lambda bi, i: (bi, i, 0))
    return pl.pallas_call(
        _in_proj_kernel,
        grid=(b, t // tm),
        in_specs=[tok(d),
                  pl.BlockSpec((None, 6, d), lambda bi, i: (bi * 2 + (i >= nlt).astype(jnp.int32), 0, 0)),
                  pl.BlockSpec((tm, LANES), lambda bi, i: (i, 0)),
                  pl.BlockSpec((tm, LANES), lambda bi, i: (i, 0)),
                  pl.BlockSpec((d, IN_COLS), lambda bi, i: (0, 0))],
        out_specs=[tok(QKV_COLS), tok(QKV_COLS), tok(QKV_COLS), tok(CONV_WIDTH), tok(CONV_WIDTH)],
        out_shape=[jax.ShapeDtypeStruct((b, t, QKV_COLS), jnp.bfloat16)] * 3
        + [jax.ShapeDtypeStruct((b, t, CONV_WIDTH), jnp.float32)] * 2,
        compiler_params=_cparams(("parallel", "parallel")),
        name="in_proj",
    )(x_all, modtab, cos_t, sin_t, w_bf16)


def _na_kernel(*refs, n_lat_blocks):
    q_ref = refs[0]
    k_refs = refs[1:2 + n_lat_blocks]
    v_refs = refs[2 + n_lat_blocks:3 + 2 * n_lat_blocks]
    if n_lat_blocks:
        bias_ref, o_ref = refs[3 + 2 * n_lat_blocks:]
    else:
        (o_ref,) = refs[3 + 2 * n_lat_blocks:]
    lane = lax.broadcasted_iota(jnp.int32, (1, LANES), 1)
    lat_keys = n_lat_blocks * TOKEN_TILE
    for pair in range(NA_HEADS // 2):
        cols = slice(pair * LANES, (pair + 1) * LANES)
        q = q_ref[:, cols]
        outs = []
        for hh in range(2):
            head = 2 * pair + hh
            in_head = (lane >= hh * HEAD_DIM) & (lane < (hh + 1) * HEAD_DIM)
            qm = jnp.where(in_head, q, jnp.zeros_like(q))
            s = jnp.concatenate(
                [lax.dot_general(qm, kr[:, cols], (((1,), (1,)), ((), ())),
                                 preferred_element_type=jnp.float32) for kr in k_refs], axis=1)
            if n_lat_blocks:
                s = jnp.concatenate([s[:, :lat_keys] + bias_ref[head], s[:, lat_keys:]], axis=1)
            m = jnp.max(s, axis=-1, keepdims=True)
            e = jnp.exp2(s - m)
            l = jnp.sum(e, axis=-1, keepdims=True)
            eb = e.astype(jnp.bfloat16)
            acc = None
            for j, vr in enumerate(v_refs):
                part = jnp.dot(eb[:, j * TOKEN_TILE:(j + 1) * TOKEN_TILE], vr[:, cols],
                               preferred_element_type=jnp.float32)
                acc = part if acc is None else acc + part
            outs.append(acc / l)
        o_ref[:, cols] = jnp.where(lane < HEAD_DIM, outs[0], outs[1]).astype(o_ref.dtype)


def _na_bias_table(rpb, rows):
    n_groups = rows // NA_ROWS
    heads = rpb.shape[0]
    padded = jnp.pad(rpb.astype(jnp.float32) * LOG2E, ((0, 0), (0, 0), (GRID_W, GRID_W)))
    toeplitz = jnp.stack([padded[:, :, GRID_W + WIN_W - 1 - qc:2 * GRID_W + WIN_W - 1 - qc]
                          for qc in range(GRID_W)], axis=2)
    qc = np.arange(GRID_W)[:, None]
    kc = np.arange(GRID_W)[None, :]
    col_start = np.clip(qc - WIN_W // 2, 0, GRID_W - WIN_W)
    col_valid = (kc >= col_start) & (kc < col_start + WIN_W)
    toeplitz = jnp.where(col_valid[None, None], toeplitz, NEG_BIG)
    masked = jnp.full((heads, GRID_W, GRID_W), NEG_BIG, jnp.float32)
    tables = []
    for g in (0, 1, n_groups - 1):
        ws = min(max(g - 1, 0), n_groups - 3) * NA_ROWS
        q_rows = []
        for qr in range(g * NA_ROWS, (g + 1) * NA_ROWS):
            row_start = min(max(qr - WIN_H // 2, 0), rows - WIN_H)
            blocks = [toeplitz[:, kr - qr + WIN_H - 1] if row_start <= kr < row_start + WIN_H else masked
                      for kr in range(ws, ws + NA_KEY_ROWS)]
            q_rows.append(jnp.concatenate(blocks, axis=-1))
        tables.append(jnp.concatenate(q_rows, axis=1))
    return jnp.stack(tables)


def _na_latent(q, k, v, bias, n_lat):
    b = q.shape[0]
    tm = TOKEN_TILE
    assert NA_ROWS * GRID_W == tm and NA_KEY_ROWS * GRID_W == 3 * tm
    ng = n_lat // tm
    ctx_blk = n_lat // tm

    def kv_spec(j):
        return pl.BlockSpec((None, tm, NA_WIDTH),
                            lambda bi, g: (bi, jnp.clip(g - 1, 0, ng - 3) + j, NA_COL_BLOCK))

    ctx_spec = pl.BlockSpec((None, tm, NA_WIDTH), lambda bi, g: (bi, ctx_blk, NA_COL_BLOCK))
    variant = lambda bi, g: ((g > 0).astype(jnp.int32) + (g == ng - 1).astype(jnp.int32), 0, 0, 0)
    return pl.pallas_call(
        functools.partial(_na_kernel, n_lat_blocks=3),
        grid=(b, ng),
        in_specs=[pl.BlockSpec((None, tm, NA_WIDTH), lambda bi, g: (bi, g, NA_COL_BLOCK)),
                  kv_spec(0), kv_spec(1), kv_spec(2), ctx_spec,
                  kv_spec(0), kv_spec(1), kv_spec(2), ctx_spec,
                  pl.BlockSpec((None, NA_HEADS, tm, 3 * tm), variant)],
        out_specs=pl.BlockSpec((None, tm, NA_WIDTH), lambda bi, g: (bi, g, 0)),
        out_shape=jax.ShapeDtypeStruct((b, n_lat, NA_WIDTH), jnp.bfloat16),
        compiler_params=_cparams(("parallel", "parallel")),
        name="na_latent",
    )(q, k, k, k, k, v, v, v, v, bias)


def _na_context(q, k, v, n_lat):
    b = q.shape[0]
    tm = TOKEN_TILE
    ctx_spec = pl.BlockSpec((None, tm, NA_WIDTH), lambda bi: (bi, n_lat // tm, NA_COL_BLOCK))
    return pl.pallas_call(
        functools.partial(_na_kernel, n_lat_blocks=0),
        grid=(b,),
        in_specs=[ctx_spec, ctx_spec, ctx_spec],
        out_specs=pl.BlockSpec((None, tm, NA_WIDTH), lambda bi: (bi, 0, 0)),
        out_shape=jax.ShapeDtypeStruct((b, tm, NA_WIDTH), jnp.bfloat16),
        compiler_params=_cparams(("parallel",)),
        name="na_context",
    )(q, k, v)


def _diff_kernel(q_ref, k_ref, v_ref, lam_ref, gain_ref, o_ref, m_ref, acc_ref, *, lambda_init):
    kk = pl.program_id(3)

    @pl.when(kk == 0)
    def _():
        m_ref[...] = jnp.full(m_ref.shape, NEG_BIG, jnp.float32)
        acc_ref[...] = jnp.zeros(acc_ref.shape, jnp.float32)

    lane = lax.broadcasted_iota(jnp.int32, (1, LANES), 1)
    reps = k_ref.shape[0] // LANES
    n_heads = q_ref.shape[1] // LANES
    for vh in range(n_heads):
        cols = slice(vh * LANES, (vh + 1) * LANES)
        q = q_ref[:, cols]
        k = k_ref[:, cols]
        v = v_ref[:, cols]
        v_ext = jnp.concatenate([v, jnp.ones_like(v)], axis=1)
        for hh in range(2):
            idx = 2 * vh + hh
            in_head = (lane >= hh * HEAD_DIM) & (lane < (hh + 1) * HEAD_DIM)
            qm = jnp.where(in_head, q, jnp.zeros_like(q))
            s = lax.dot_general(qm, k, (((1,), (1,)), ((), ())), preferred_element_type=jnp.float32)
            m_old = m_ref[idx]
            m_new = jnp.maximum(m_old, jnp.max(s, axis=-1, keepdims=True))
            alpha = jnp.exp2(m_old - m_new)
            p = jnp.exp2(s - jnp.concatenate([m_new] * reps, axis=1)).astype(jnp.bfloat16)
            pv = jnp.dot(p, v_ext, preferred_element_type=jnp.float32)
            acc_ref[idx] = jnp.concatenate([alpha, alpha], axis=1) * acc_ref[idx] + pv
            m_ref[idx] = m_new

    @pl.when(kk == pl.num_programs(3) - 1)
    def _():
        lp = lam_ref[...]
        lam = (jnp.exp(jnp.sum(lp[0:1] * lp[1:2], axis=-1, keepdims=True))
               - jnp.exp(jnp.sum(lp[2:3] * lp[3:4], axis=-1, keepdims=True)) + lambda_init)
        for vh in range(n_heads):
            a0 = acc_ref[2 * vh]
            a1 = acc_ref[2 * vh + 1]
            o = a0[:, :LANES] / a0[:, LANES:] - lam * (a1[:, :LANES] / a1[:, LANES:])
            o_ref[:, vh * LANES:(vh + 1) * LANES] = (
                _rms(o) * gain_ref[...] * (1.0 - lambda_init)).astype(o_ref.dtype)


def _diff_attention(q, k, v, lam_params, gain, lambda_init, *, q_rows, q_start, tq, k_rows, k_start, tk,
                    heads_per_step):
    b = q.shape[0]
    nq, nk = q_rows // tq, k_rows // tk
    qo, ko = q_start // tq, k_start // tk
    width = heads_per_step * LANES
    first = 0
    return pl.pallas_call(
        functools.partial(_diff_kernel, lambda_init=lambda_init),
        grid=(b, DIFF_HEADS // heads_per_step, nq, nk),
        in_specs=[pl.BlockSpec((None, tq, width), lambda bi, h, i, j: (bi, qo + i, first + h)),
                  pl.BlockSpec((None, tk, width), lambda bi, h, i, j: (bi, ko + j, first + h)),
                  pl.BlockSpec((None, tk, width), lambda bi, h, i, j: (bi, ko + j, first + h)),
                  pl.BlockSpec((4, HEAD_DIM), lambda bi, h, i, j: (0, 0)),
                  pl.BlockSpec((1, LANES), lambda bi, h, i, j: (0, 0))],
        out_specs=pl.BlockSpec((None, tq, width), lambda bi, h, i, j: (bi, i, h)),
        out_shape=jax.ShapeDtypeStruct((b, q_rows, DIFF_WIDTH), jnp.bfloat16),
        scratch_shapes=[pltpu.VMEM((2 * heads_per_step, tq, LANES), jnp.float32),
                        pltpu.VMEM((2 * heads_per_step, tq, 2 * LANES), jnp.float32)],
        compiler_params=_cparams(("parallel", "parallel", "parallel", "arbitrary")),
        name="diff_attention",
    )(q, k, v, lam_params, gain)


def _out_proj_kernel(x_ref, na_ref, dif_ref, u_ref, up_ref, un_ref, bg_ref, cw_ref, w_ref, mod_ref,
                     *rest, n_lat_tiles, router):
    if router:
        rw_ref, x1_ref, h2_ref, gates_ref = rest
    else:
        x1_ref, h2_ref = rest
    i = pl.program_id(1)
    tm = x_ref.shape[0]
    first_of_seq = (i == 0) | (i == n_lat_tiles)
    last_of_seq = (i == n_lat_tiles - 1) | (i == n_lat_tiles)
    u = u_ref[...]
    up = jnp.where(first_of_seq, 0.0, up_ref[7:8, :])
    un = jnp.where(last_of_seq, 0.0, un_ref[0:1, :])
    row = lax.broadcasted_iota(jnp.int32, (tm, 1), 0)
    u_prev = jnp.where(row == 0, up, pltpu.roll(u, 1, axis=0))
    u_next = jnp.where(row == tm - 1, un, pltpu.roll(u, tm - 1, axis=0))
    conv = bg_ref[...] * (cw_ref[0:1, :] * u_prev + cw_ref[1:2, :] * u + cw_ref[2:3, :] * u_next)
    o1 = NA_WIDTH
    o2 = NA_WIDTH + DIFF_WIDTH
    mix = (jnp.dot(na_ref[...], w_ref[:o1, :], preferred_element_type=jnp.float32)
           + jnp.dot(dif_ref[...], w_ref[o1:o2, :], preferred_element_type=jnp.float32)
           + jnp.dot(conv.astype(jnp.bfloat16), w_ref[o2:, :], preferred_element_type=jnp.float32))
    x1 = x_ref[...] + mod_ref[2:3, :] * mix
    x1_ref[...] = x1
    h2 = _rms(x1) * (1.0 + mod_ref[4:5, :]) + mod_ref[3:4, :]
    h2_ref[...] = h2.astype(h2_ref.dtype)
    if router:
        logits = jnp.dot(h2, rw_ref[...], preferred_element_type=jnp.float32,
                         precision=lax.Precision.HIGHEST)
        lane = lax.broadcasted_iota(jnp.int32, logits.shape, 1)
        logits = jnp.where(lane < N_EXPERTS, logits, NEG_BIG)
        m1 = jnp.max(logits, axis=-1, keepdims=True)
        i1 = jnp.min(jnp.where(logits == m1, lane, LANES), axis=-1, keepdims=True)
        rest_l = jnp.where(lane == i1, NEG_BIG, logits)
        m2 = jnp.max(rest_l, axis=-1, keepdims=True)
        i2 = jnp.min(jnp.where(rest_l == m2, lane, LANES), axis=-1, keepdims=True)
        e2 = jnp.exp(m2 - m1)
        w1 = 1.0 / (1.0 + e2)
        gates_ref[...] = jnp.where(lane == i1, w1, jnp.where(lane == i2, e2 * w1, 0.0))


def _out_proj(x_all, na, dif, u, bg, conv_w, w_bf16, modtab, n_lat, n_rows, router_w=None):
    b, t, d = x_all.shape
    tm = TOKEN_TILE
    nlt = n_lat // tm
    sub = 8
    n_sub = t // sub
    tok = lambda width: pl.BlockSpec((None, tm, width), lambda bi, i: (bi, i, 0))
    in_specs = [tok(d), tok(NA_WIDTH), tok(DIFF_WIDTH), tok(CONV_WIDTH),
                pl.BlockSpec((None, sub, CONV_WIDTH),
                             lambda bi, i: (bi, jnp.maximum(i * (tm // sub) - 1, 0), 0)),
                pl.BlockSpec((None, sub, CONV_WIDTH),
                             lambda bi, i: (bi, jnp.minimum((i + 1) * (tm // sub), n_sub - 1), 0)),
                tok(CONV_WIDTH),
                pl.BlockSpec((3, CONV_WIDTH), lambda bi, i: (0, 0)),
                pl.BlockSpec((d, d), lambda bi, i: (0, 0)),
                pl.BlockSpec((None, 6, d), lambda bi, i: (bi * 2 + (i >= nlt).astype(jnp.int32), 0, 0))]
    out_specs = [tok(d), tok(d)]
    out_shape = [jax.ShapeDtypeStruct((b, n_rows, d), jnp.float32),
                 jax.ShapeDtypeStruct((b, n_rows, d), jnp.bfloat16)]
    args = [x_all, na, dif, u, u, u, bg, conv_w, w_bf16, modtab]
    if router_w is not None:
        in_specs.append(pl.BlockSpec((d, LANES), lambda bi, i: (0, 0)))
        out_specs.append(tok(LANES))
        out_shape.append(jax.ShapeDtypeStruct((b, n_rows, LANES), jnp.float32))
        args.append(router_w)
    return pl.pallas_call(
        functools.partial(_out_proj_kernel, n_lat_tiles=nlt, router=router_w is not None),
        grid=(b, n_rows // tm),
        in_specs=in_specs, out_specs=out_specs, out_shape=out_shape,
        compiler_params=_cparams(("parallel", "parallel")),
        name="out_proj",
    )(*args)


def _ffn_kernel(x1_ref, h_ref, w1_ref, w3_ref, w2_ref, mod_ref, o_ref):
    h = h_ref[...]
    a = jnp.dot(h, w1_ref[...], preferred_element_type=jnp.float32)
    g = jnp.dot(h, w3_ref[...], preferred_element_type=jnp.float32)
    act = (a * jax.nn.sigmoid(a) * g).astype(jnp.bfloat16)
    y = jnp.dot(act, w2_ref[...], preferred_element_type=jnp.float32)
    o_ref[...] = x1_ref[...] + mod_ref[5:6, :] * y


def _dense_ffn(x1, h2, w1, w3, w2, modtab, n_lat):
    b, t, d = x1.shape
    tm = TOKEN_TILE
    nlt = n_lat // tm
    ff = w1.shape[1]
    tok = pl.BlockSpec((None, tm, d), lambda bi, i: (bi, i, 0))
    const = lambda shape: pl.BlockSpec(shape, lambda bi, i: (0, 0), pipeline_mode=pl.Buffered(1))
    return pl.pallas_call(
        _ffn_kernel,
        grid=(b, t // tm),
        in_specs=[tok, tok, const((d, ff)), const((d, ff)), const((ff, d)),
                  pl.BlockSpec((None, 6, d), lambda bi, i: (bi * 2 + (i >= nlt).astype(jnp.int32), 0, 0))],
        out_specs=tok,
        out_shape=jax.ShapeDtypeStruct((b, t, d), jnp.float32),
        compiler_params=_cparams(("parallel", "parallel")),
        name="dense_ffn",
    )(x1, h2, w1, w3, w2, modtab)


def _moe_kernel(x1_ref, h_ref, gates_ref, w1_ref, w3_ref, w2_ref, mod_ref, gain_ref, o_ref, acc_ref):
    e = pl.program_id(2)
    c = pl.program_id(3)

    @pl.when((e == 0) & (c == 0))
    def _():
        acc_ref[...] = jnp.zeros(acc_ref.shape, jnp.float32)

    lane = lax.broadcasted_iota(jnp.int32, gates_ref.shape, 1)
    gate = jnp.sum(jnp.where(lane == e, gates_ref[...], 0.0), axis=-1, keepdims=True)
    h = h_ref[...]
    a = jnp.dot(h, w1_ref[...], preferred_element_type=jnp.float32)
    g = jnp.dot(h, w3_ref[...], preferred_element_type=jnp.float32)
    act = (a * jax.nn.sigmoid(a) * g * gate).astype(jnp.bfloat16)
    acc_ref[...] += jnp.dot(act, w2_ref[...], preferred_element_type=jnp.float32)

    @pl.when((e == pl.num_programs(2) - 1) & (c == pl.num_programs(3) - 1))
    def _():
        x2 = x1_ref[...] + mod_ref[5:6, :] * acc_ref[...]
        o_ref[...] = _rms(x2) * gain_ref[...]


def _moe_ffn_final(x1, h2, gates, w1, w3, w2, modtab, final_gain):
    b, n, d = x1.shape
    tm = 512
    n_exp, _, ff = w1.shape
    fc = ff // 2
    tok = lambda width: pl.BlockSpec((None, tm, width), lambda bi, i, e, c: (bi, i, 0))
    return pl.pallas_call(
        _moe_kernel,
        grid=(b, n // tm, n_exp, ff // fc),
        in_specs=[tok(d), tok(d), tok(LANES),
                  pl.BlockSpec((None, d, fc), lambda bi, i, e, c: (e, 0, c)),
                  pl.BlockSpec((None, d, fc), lambda bi, i, e, c: (e, 0, c)),
                  pl.BlockSpec((None, fc, d), lambda bi, i, e, c: (e, c, 0)),
                  pl.BlockSpec((None, 6, d), lambda bi, i, e, c: (bi * 2, 0, 0)),
                  pl.BlockSpec((1, d), lambda bi, i, e, c: (0, 0))],
        out_specs=tok(d),
        out_shape=jax.ShapeDtypeStruct((b, n, d), jnp.float32),
        scratch_shapes=[pltpu.VMEM((tm, d), jnp.float32)],
        compiler_params=_cparams(("parallel", "parallel", "arbitrary", "arbitrary")),
        name="moe_ffn",
    )(x1, h2, gates, w1, w3, w2, modtab, final_gain.reshape(1, d))


def _rope_tables(n_lat, n_ctx):
    t = jnp.arange(n_lat, dtype=jnp.int32)
    n_freq = HEAD_DIM // 4
    inv_freq = ROPE_BASE ** (-jnp.arange(n_freq, dtype=jnp.float32) / n_freq)
    ang = jnp.concatenate([(t // GRID_W).astype(jnp.float32)[:, None] * inv_freq,
                           (t % GRID_W).astype(jnp.float32)[:, None] * inv_freq], axis=-1)
    cos, sin = jnp.cos(ang), jnp.sin(ang)
    reps = LANES // HEAD_DIM
    cos_t = jnp.tile(jnp.concatenate([cos, cos], axis=-1), (1, reps))
    sin_t = jnp.tile(jnp.concatenate([-sin, sin], axis=-1), (1, reps))
    cos_t = jnp.concatenate([cos_t, jnp.ones((n_ctx, LANES), jnp.float32)], axis=0)
    sin_t = jnp.concatenate([sin_t, jnp.zeros((n_ctx, LANES), jnp.float32)], axis=0)
    return cos_t, sin_t


def kernel(x, c, ctx, c_ctx, ada_w, ada_b, w_in, w_out, na_rpb, diff_lambda, diff_subln, conv_w,
           ffn_w1, ffn_w3, ffn_w2, router_w, moe_w1, moe_w3, moe_w2, final_gain):
    b, n, d = x.shape
    n_ctx = ctx.shape[1]
    depth = w_in.shape[0]
    assert d == D_MODEL and n % TOKEN_TILE == 0 and n_ctx == TOKEN_TILE and b + 1 <= 8
    assert depth == 2, "layer 0 dense with a context stream, layer 1 routed and final"
    bf = jnp.bfloat16

    x_all = jnp.concatenate([x, ctx], axis=1)
    cvec = jnp.zeros((8, d), jnp.float32).at[:b].set(c).at[b].set(c_ctx)
    mod = _modulation(cvec, ada_w, ada_b).reshape(depth, 8, 6, d)
    cos_t, sin_t = _rope_tables(n, n_ctx)
    rows = n // GRID_W

    out = None
    for i in range(depth):
        lambda_init = 0.8 - 0.6 * math.exp(-0.3 * i)
        ctx_out = i < depth - 1
        modtab = jnp.stack([mod[i, :b], jnp.broadcast_to(mod[i, b], (b, 6, d))], axis=1).reshape(2 * b, 6, d)
        q, k, v, u, bg = _in_proj(x_all, modtab, cos_t, sin_t, w_in[i].astype(bf), n)
        bias = _na_bias_table(na_rpb[i], rows)
        na = _na_latent(q, k, v, bias, n)
        gain = diff_subln[i].reshape(1, 2 * HEAD_DIM)
        dif = _diff_attention(q, k, v, diff_lambda[i], gain, lambda_init,
                              q_rows=n, q_start=0, tq=512, k_rows=n + n_ctx, k_start=0, tk=768,
                              heads_per_step=4)
        if ctx_out:
            na_c = _na_context(q, k, v, n)
            dif_c = _diff_attention(q, k, v, diff_lambda[i], gain, lambda_init,
                                    q_rows=n_ctx, q_start=n, tq=n_ctx, k_rows=n_ctx, k_start=n, tk=n_ctx,
                                    heads_per_step=4)
            na = jnp.concatenate([na, na_c], axis=1)
            dif = jnp.concatenate([dif, dif_c], axis=1)
            x1, h2 = _out_proj(x_all, na, dif, u, bg, conv_w[i], w_out[i].astype(bf), modtab, n, n + n_ctx)
            m = i // 2
            x_all = _dense_ffn(x1, h2, ffn_w1[m].astype(bf), ffn_w3[m].astype(bf), ffn_w2[m].astype(bf),
                               modtab, n)
        else:
            m = i // 2
            rw = jnp.zeros((d, LANES), jnp.float32).at[:, :N_EXPERTS].set(router_w[m])
            x1, h2, gates = _out_proj(x_all, na, dif, u, bg, conv_w[i], w_out[i].astype(bf), modtab, n, n,
                                      router_w=rw)
            out = _moe_ffn_final(x1, h2, gates, moe_w1[m].astype(bf), moe_w3[m].astype(bf),
                                 moe_w2[m].astype(bf), modtab, final_gain)
    return out
```

```python
import functools
import math

import numpy as np
import jax
import jax.numpy as jnp
from jax import lax
from jax.experimental import pallas as pl
from jax.experimental.pallas import tpu as pltpu

D_MODEL = 1024
GRID_W = 64
HEAD_DIM = 64
NA_HEADS = 4
NA_WIDTH = NA_HEADS * HEAD_DIM
WIN_H = 8
WIN_W = 16
DIFF_HEADS = 4
DIFF_WIDTH = DIFF_HEADS * 2 * HEAD_DIM
CONV_WIDTH = 256
QKV_COLS = NA_WIDTH + DIFF_WIDTH
NA_COL_BLOCK = DIFF_WIDTH // NA_WIDTH
IN_COLS = 3 * QKV_COLS + 3 * CONV_WIDTH
ROPE_BASE = 10000.0
N_EXPERTS = 8
EPS = 1e-6
NEG_BIG = -1e30
LOG2E = math.log2(math.e)

LANES = 128
TOKEN_TILE = 256
NA_ROWS = 4
NA_KEY_ROWS = NA_ROWS + WIN_H
VMEM_LIMIT = 56 * 1024 * 1024


def _cparams(sem):
    return pltpu.CompilerParams(dimension_semantics=sem, vmem_limit_bytes=VMEM_LIMIT)


def _rms(x):
    return x * lax.rsqrt(jnp.mean(x * x, axis=-1, keepdims=True) + EPS)


def _pack_halves(x):
    half = x.shape[1] // 2
    return pltpu.pack_elementwise([x[:, :half], x[:, half:]], packed_dtype=jnp.bfloat16)


def _unpack_halves(p):
    parts = [pltpu.unpack_elementwise(p, index=i, packed_dtype=jnp.bfloat16, unpacked_dtype=jnp.float32)
             for i in range(2)]
    return jnp.concatenate(parts, axis=1)


def _mod_kernel(c_ref, w_ref, b_ref, o_ref):
    cv = c_ref[...]
    s = cv * jax.nn.sigmoid(cv)
    o_ref[...] = jnp.dot(s, w_ref[...], preferred_element_type=jnp.float32,
                         precision=lax.Precision.HIGHEST) + b_ref[...]


def _modulation(cvec, ada_w, ada_b):
    depth, d, cols = ada_w.shape
    cb = 1536
    return pl.pallas_call(
        _mod_kernel,
        grid=(depth, cols // cb),
        in_specs=[pl.BlockSpec((8, d), lambda i, j: (0, 0)),
                  pl.BlockSpec((None, d, cb), lambda i, j: (i, 0, j)),
                  pl.BlockSpec((None, 1, cb), lambda i, j: (i, 0, j))],
        out_specs=pl.BlockSpec((None, 8, cb), lambda i, j: (i, 0, j)),
        out_shape=jax.ShapeDtypeStruct((depth, 8, cols), jnp.float32),
        compiler_params=_cparams(("parallel", "parallel")),
        name="modulation",
    )(cvec, ada_w, ada_b.reshape(depth, 1, cols))


def _rope(z, cos, sin_signed):
    width = z.shape[1]
    reps = width // LANES
    c = jnp.concatenate([cos] * reps, axis=1)
    s = jnp.concatenate([sin_signed] * reps, axis=1)
    lane = lax.broadcasted_iota(jnp.int32, (1, width), 1)
    first_half = (lane % HEAD_DIM) < (HEAD_DIM // 2)
    swapped = jnp.where(first_half,
                        pltpu.roll(z, width - HEAD_DIM // 2, axis=1),
                        pltpu.roll(z, HEAD_DIM // 2, axis=1))
    return z * c + swapped * s


def _in_proj_kernel(x_ref, mod_ref, cos_ref, sin_ref, w_ref, q_ref, k_ref, v_ref, u_ref, bg_ref):
    x = x_ref[...]
    h = _rms(x) * (1.0 + mod_ref[1:2, :]) + mod_ref[0:1, :]
    p = jnp.dot(h.astype(jnp.bfloat16), w_ref[...], preferred_element_type=jnp.float32)
    cos = cos_ref[...]
    sin = sin_ref[...]
    scale = HEAD_DIM ** -0.5 * LOG2E
    q_ref[:, :DIFF_WIDTH] = (_rope(p[:, NA_WIDTH:QKV_COLS], cos, sin) * scale).astype(q_ref.dtype)
    q_ref[:, DIFF_WIDTH:] = (p[:, :NA_WIDTH] * scale).astype(q_ref.dtype)
    k_ref[:, :DIFF_WIDTH] = _rope(p[:, QKV_COLS + NA_WIDTH:2 * QKV_COLS], cos, sin).astype(k_ref.dtype)
    k_ref[:, DIFF_WIDTH:] = p[:, QKV_COLS:QKV_COLS + NA_WIDTH].astype(k_ref.dtype)
    o = 2 * QKV_COLS
    v_ref[:, :DIFF_WIDTH] = p[:, o + NA_WIDTH:o + QKV_COLS].astype(v_ref.dtype)
    v_ref[:, DIFF_WIDTH:] = p[:, o:o + NA_WIDTH].astype(v_ref.dtype)
    o = 3 * QKV_COLS
    xin = p[:, o:o + CONV_WIDTH]
    bg_ref[...] = p[:, o + CONV_WIDTH:o + 2 * CONV_WIDTH]
    u_ref[...] = p[:, o + 2 * CONV_WIDTH:o + 3 * CONV_WIDTH] * xin


def _in_proj(x_all, modtab, cos_t, sin_t, w_bf16, n_lat):
    b, t, d = x_all.shape
    tm = TOKEN_TILE
    nlt = n_lat // tm
    tok = lambda width: pl.BlockSpec((None, tm, width), lambda bi, i: (bi, i, 0))
    return pl.pallas_call(
        _in_proj_kernel,
        grid=(b, t // tm),
        in_specs=[tok(d),
                  pl.BlockSpec((None, 6, d), lambda bi, i: (bi * 2 + (i >= nlt).astype(jnp.int32), 0, 0)),
                  pl.BlockSpec((tm, LANES), lambda bi, i: (i, 0)),
                  pl.BlockSpec((tm, LANES), lambda bi, i: (i, 0)),
                  pl.BlockSpec((d, IN_COLS), lambda bi, i: (0, 0))],
        out_specs=[tok(QKV_COLS), tok(QKV_COLS), tok(QKV_COLS), tok(CONV_WIDTH), tok(CONV_WIDTH)],
        out_shape=[jax.ShapeDtypeStruct((b, t, QKV_COLS), jnp.bfloat16)] * 3
        + [jax.ShapeDtypeStruct((b, t, CONV_WIDTH), jnp.float32)] * 2,
        compiler_params=_cparams(("parallel", "parallel")),
        name="in_proj",
    )(x_all, modtab, cos_t, sin_t, w_bf16)


def _na_kernel(*refs, n_lat_blocks):
    q_ref = refs[0]
    k_refs = refs[1:2 + n_lat_blocks]
    v_refs = refs[2 + n_lat_blocks:3 + 2 * n_lat_blocks]
    if n_lat_blocks:
        bias_ref, o_ref = refs[3 + 2 * n_lat_blocks:]
    else:
        (o_ref,) = refs[3 + 2 * n_lat_blocks:]
    lane = lax.broadcasted_iota(jnp.int32, (1, LANES), 1)
    lat_keys = n_lat_blocks * TOKEN_TILE
    for pair in range(NA_HEADS // 2):
        cols = slice(pair * LANES, (pair + 1) * LANES)
        q = q_ref[:, cols]
        outs = []
        for hh in range(2):
            head = 2 * pair + hh
            in_head = (lane >= hh * HEAD_DIM) & (lane < (hh + 1) * HEAD_DIM)
            qm = jnp.where(in_head, q, jnp.zeros_like(q))
            s = jnp.concatenate(
                [lax.dot_general(qm, kr[:, cols], (((1,), (1,)), ((), ())),
                                 preferred_element_type=jnp.float32) for kr in k_refs], axis=1)
            if n_lat_blocks:
                s = jnp.concatenate([s[:, :lat_keys] + bias_ref[head], s[:, lat_keys:]], axis=1)
            m = jnp.max(s, axis=-1, keepdims=True)
            e = jnp.exp2(s - m)
            l = jnp.sum(e, axis=-1, keepdims=True)
            eb = e.astype(jnp.bfloat16)
            acc = None
            for j, vr in enumerate(v_refs):
                part = jnp.dot(eb[:, j * TOKEN_TILE:(j + 1) * TOKEN_TILE], vr[:, cols],
                               preferred_element_type=jnp.float32)
                acc = part if acc is None else acc + part
            outs.append(acc / l)
        o_ref[:, cols] = jnp.where(lane < HEAD_DIM, outs[0], outs[1]).astype(o_ref.dtype)


def _na_bias_table(rpb, rows):
    n_groups = rows // NA_ROWS
    heads = rpb.shape[0]
    padded = jnp.pad(rpb.astype(jnp.float32) * LOG2E, ((0, 0), (0, 0), (GRID_W, GRID_W)))
    toeplitz = jnp.stack([padded[:, :, GRID_W + WIN_W - 1 - qc:2 * GRID_W + WIN_W - 1 - qc]
                          for qc in range(GRID_W)], axis=2)
    qc = np.arange(GRID_W)[:, None]
    kc = np.arange(GRID_W)[None, :]
    col_start = np.clip(qc - WIN_W // 2, 0, GRID_W - WIN_W)
    col_valid = (kc >= col_start) & (kc < col_start + WIN_W)
    toeplitz = jnp.where(col_valid[None, None], toeplitz, NEG_BIG)
    masked = jnp.full((heads, GRID_W, GRID_W), NEG_BIG, jnp.float32)
    tables = []
    for g in (0, 1, n_groups - 1):
        ws = min(max(g - 1, 0), n_groups - 3) * NA_ROWS
        q_rows = []
        for qr in range(g * NA_ROWS, (g + 1) * NA_ROWS):
            row_start = min(max(qr - WIN_H // 2, 0), rows - WIN_H)
            blocks = [toeplitz[:, kr - qr + WIN_H - 1] if row_start <= kr < row_start + WIN_H else masked
                      for kr in range(ws, ws + NA_KEY_ROWS)]
            q_rows.append(jnp.concatenate(blocks, axis=-1))
        tables.append(jnp.concatenate(q_rows, axis=1))
    return jnp.stack(tables)


def _na_latent(q, k, v, bias, n_lat):
    b = q.shape[0]
    tm = TOKEN_TILE
    assert NA_ROWS * GRID_W == tm and NA_KEY_ROWS * GRID_W == 3 * tm
    ng = n_lat // tm
    ctx_blk = n_lat // tm

    def kv_spec(j):
        return pl.BlockSpec((None, tm, NA_WIDTH),
                            lambda bi, g: (bi, jnp.clip(g - 1, 0, ng - 3) + j, NA_COL_BLOCK))

    ctx_spec = pl.BlockSpec((None, tm, NA_WIDTH), lambda bi, g: (bi, ctx_blk, NA_COL_BLOCK))
    variant = lambda bi, g: ((g > 0).astype(jnp.int32) + (g == ng - 1).astype(jnp.int32), 0, 0, 0)
    return pl.pallas_call(
        functools.partial(_na_kernel, n_lat_blocks=3),
        grid=(b, ng),
        in_specs=[pl.BlockSpec((None, tm, NA_WIDTH), lambda bi, g: (bi, g, NA_COL_BLOCK)),
                  kv_spec(0), kv_spec(1), kv_spec(2), ctx_spec,
                  kv_spec(0), kv_spec(1), kv_spec(2), ctx_spec,
                  pl.BlockSpec((None, NA_HEADS, tm, 3 * tm), variant)],
        out_specs=pl.BlockSpec((None, tm, NA_WIDTH), lambda bi, g: (bi, g, 0)),
        out_shape=jax.ShapeDtypeStruct((b, n_lat, NA_WIDTH), jnp.bfloat16),
        compiler_params=_cparams(("parallel", "parallel")),
        name="na_latent",
    )(q, k, k, k, k, v, v, v, v, bias)


def _na_context(q, k, v, n_lat):
    b = q.shape[0]
    tm = TOKEN_TILE
    ctx_spec = pl.BlockSpec((None, tm, NA_WIDTH), lambda bi: (bi, n_lat // tm, NA_COL_BLOCK))
    return pl.pallas_call(
        functools.partial(_na_kernel, n_lat_blocks=0),
        grid=(b,),
        in_specs=[ctx_spec, ctx_spec, ctx_spec],
        out_specs=pl.BlockSpec((None, tm, NA_WIDTH), lambda bi: (bi, 0, 0)),
        out_shape=jax.ShapeDtypeStruct((b, tm, NA_WIDTH), jnp.bfloat16),
        compiler_params=_cparams(("parallel",)),
        name="na_context",
    )(q, k, v)


def _diff_kernel(q_ref, k_ref, v_ref, lam_ref, gain_ref, o_ref, m_ref, acc_ref, *, lambda_init):
    kk = pl.program_id(3)

    @pl.when(kk == 0)
    def _():
        m_ref[...] = jnp.full(m_ref.shape, NEG_BIG, jnp.float32)
        acc_ref[...] = jnp.zeros(acc_ref.shape, jnp.float32)

    lane = lax.broadcasted_iota(jnp.int32, (1, LANES), 1)
    reps = k_ref.shape[0] // LANES
    n_heads = q_ref.shape[1] // LANES
    for vh in range(n_heads):
        cols = slice(vh * LANES, (vh + 1) * LANES)
        q = q_ref[:, cols]
        k = k_ref[:, cols]
        v = v_ref[:, cols]
        v_ext = jnp.concatenate([v, jnp.ones_like(v)], axis=1)
        for hh in range(2):
            idx = 2 * vh + hh
            in_head = (lane >= hh * HEAD_DIM) & (lane < (hh + 1) * HEAD_DIM)
            qm = jnp.where(in_head, q, jnp.zeros_like(q))
            s = lax.dot_general(qm, k, (((1,), (1,)), ((), ())), preferred_element_type=jnp.float32)
            m_old = m_ref[idx]
            m_new = jnp.maximum(m_old, jnp.max(s, axis=-1, keepdims=True))
            alpha = jnp.exp2(m_old - m_new)
            p = jnp.exp2(s - jnp.concatenate([m_new] * reps, axis=1)).astype(jnp.bfloat16)
            pv = jnp.dot(p, v_ext, preferred_element_type=jnp.float32)
            acc_ref[idx] = jnp.concatenate([alpha, alpha], axis=1) * acc_ref[idx] + pv
            m_ref[idx] = m_new

    @pl.when(kk == pl.num_programs(3) - 1)
    def _():
        lp = lam_ref[...]
        lam = (jnp.exp(jnp.sum(lp[0:1] * lp[1:2], axis=-1, keepdims=True))
               - jnp.exp(jnp.sum(lp[2:3] * lp[3:4], axis=-1, keepdims=True)) + lambda_init)
        for vh in range(n_heads):
            a0 = acc_ref[2 * vh]
            a1 = acc_ref[2 * vh + 1]
            o = a0[:, :LANES] / a0[:, LANES:] - lam * (a1[:, :LANES] / a1[:, LANES:])
            o_ref[:, vh * LANES:(vh + 1) * LANES] = (
                _rms(o) * gain_ref[...] * (1.0 - lambda_init)).astype(o_ref.dtype)


def _diff_attention(q, k, v, lam_params, gain, lambda_init, *, q_rows, q_start, tq, k_rows, k_start, tk,
                    heads_per_step):
    b = q.shape[0]
    nq, nk = q_rows // tq, k_rows // tk
    qo, ko = q_start // tq, k_start // tk
    width = heads_per_step * LANES
    first = 0
    return pl.pallas_call(
        functools.partial(_diff_kernel, lambda_init=lambda_init),
        grid=(b, DIFF_HEADS // heads_per_step, nq, nk),
        in_specs=[pl.BlockSpec((None, tq, width), lambda bi, h, i, j: (bi, qo + i, first + h)),
                  pl.BlockSpec((None, tk, width), lambda bi, h, i, j: (bi, ko + j, first + h)),
                  pl.BlockSpec((None, tk, width), lambda bi, h, i, j: (bi, ko + j, first + h)),
                  pl.BlockSpec((4, HEAD_DIM), lambda bi, h, i, j: (0, 0)),
                  pl.BlockSpec((1, LANES), lambda bi, h, i, j: (0, 0))],
        out_specs=pl.BlockSpec((None, tq, width), lambda bi, h, i, j: (bi, i, h)),
        out_shape=jax.ShapeDtypeStruct((b, q_rows, DIFF_WIDTH), jnp.bfloat16),
        scratch_shapes=[pltpu.VMEM((2 * heads_per_step, tq, LANES), jnp.float32),
                        pltpu.VMEM((2 * heads_per_step, tq, 2 * LANES), jnp.float32)],
        compiler_params=_cparams(("parallel", "parallel", "parallel", "arbitrary")),
        name="diff_attention",
    )(q, k, v, lam_params, gain)


def _out_proj_kernel(x_ref, na_ref, dif_ref, u_ref, up_ref, un_ref, bg_ref, cw_ref, w_ref, mod_ref,
                     *rest, n_lat_tiles, router):
    if router:
        rw_ref, x1_ref, h2_ref, route_ref = rest
    else:
        x1_ref, h2_ref = rest
    i = pl.program_id(1)
    tm = x_ref.shape[0]
    first_of_seq = (i == 0) | (i == n_lat_tiles)
    last_of_seq = (i == n_lat_tiles - 1) | (i == n_lat_tiles)
    u = u_ref[...]
    up = jnp.where(first_of_seq, 0.0, up_ref[7:8, :])
    un = jnp.where(last_of_seq, 0.0, un_ref[0:1, :])
    row = lax.broadcasted_iota(jnp.int32, (tm, 1), 0)
    u_prev = jnp.where(row == 0, up, pltpu.roll(u, 1, axis=0))
    u_next = jnp.where(row == tm - 1, un, pltpu.roll(u, tm - 1, axis=0))
    conv = bg_ref[...] * (cw_ref[0:1, :] * u_prev + cw_ref[1:2, :] * u + cw_ref[2:3, :] * u_next)
    o1 = NA_WIDTH
    o2 = NA_WIDTH + DIFF_WIDTH
    mix = (jnp.dot(na_ref[...], w_ref[:o1, :], preferred_element_type=jnp.float32)
           + jnp.dot(dif_ref[...], w_ref[o1:o2, :], preferred_element_type=jnp.float32)
           + jnp.dot(conv.astype(jnp.bfloat16), w_ref[o2:, :], preferred_element_type=jnp.float32))
    x1 = x_ref[...] + mod_ref[2:3, :] * mix
    x1_ref[...] = x1
    h2 = _rms(x1) * (1.0 + mod_ref[4:5, :]) + mod_ref[3:4, :]
    if not router:
        h2_ref[...] = h2.astype(h2_ref.dtype)
    else:
        h2_ref[...] = _pack_halves(h2)
        logits = jnp.dot(h2, rw_ref[...], preferred_element_type=jnp.float32,
                         precision=lax.Precision.HIGHEST)
        lane = lax.broadcasted_iota(jnp.int32, logits.shape, 1)
        logits = jnp.where(lane < N_EXPERTS, logits, NEG_BIG)
        m1 = jnp.max(logits, axis=-1, keepdims=True)
        i1 = jnp.min(jnp.where(logits == m1, lane, LANES), axis=-1, keepdims=True)
        rest_l = jnp.where(lane == i1, NEG_BIG, logits)
        m2 = jnp.max(rest_l, axis=-1, keepdims=True)
        i2 = jnp.min(jnp.where(rest_l == m2, lane, LANES), axis=-1, keepdims=True)
        e2 = jnp.exp(m2 - m1)
        w1 = 1.0 / (1.0 + e2)
        route_ref[...] = jnp.where(
            lane == 0, i1.astype(jnp.float32),
            jnp.where(lane == 1, i2.astype(jnp.float32),
                      jnp.where(lane == 2, w1, jnp.where(lane == 3, e2 * w1, 0.0))))


def _out_proj(x_all, na, dif, u, bg, conv_w, w_bf16, modtab, n_lat, n_rows, router_w=None):
    b, t, d = x_all.shape
    tm = TOKEN_TILE
    nlt = n_lat // tm
    sub = 8
    n_sub = t // sub
    tok = lambda width: pl.BlockSpec((None, tm, width), lambda bi, i: (bi, i, 0))
    in_specs = [tok(d), tok(NA_WIDTH), tok(DIFF_WIDTH), tok(CONV_WIDTH),
                pl.BlockSpec((None, sub, CONV_WIDTH),
                             lambda bi, i: (bi, jnp.maximum(i * (tm // sub) - 1, 0), 0)),
                pl.BlockSpec((None, sub, CONV_WIDTH),
                             lambda bi, i: (bi, jnp.minimum((i + 1) * (tm // sub), n_sub - 1), 0)),
                tok(CONV_WIDTH),
                pl.BlockSpec((3, CONV_WIDTH), lambda bi, i: (0, 0)),
                pl.BlockSpec((d, d), lambda bi, i: (0, 0)),
                pl.BlockSpec((None, 6, d), lambda bi, i: (bi * 2 + (i >= nlt).astype(jnp.int32), 0, 0))]
    args = [x_all, na, dif, u, u, u, bg, conv_w, w_bf16, modtab]
    if router_w is None:
        out_specs = [tok(d), tok(d)]
        out_shape = [jax.ShapeDtypeStruct((b, n_rows, d), jnp.float32),
                     jax.ShapeDtypeStruct((b, n_rows, d), jnp.bfloat16)]
    else:
        in_specs.append(pl.BlockSpec((d, LANES), lambda bi, i: (0, 0)))
        steps = n_rows // tm
        out_specs = [tok(d), pl.BlockSpec((tm, d // 2), lambda bi, i: (bi * steps + i, 0)), tok(LANES)]
        out_shape = [jax.ShapeDtypeStruct((b, n_rows, d), jnp.float32),
                     jax.ShapeDtypeStruct((b * n_rows, d // 2), jnp.uint32),
                     jax.ShapeDtypeStruct((b, n_rows, LANES), jnp.float32)]
        args.append(router_w)
    return pl.pallas_call(
        functools.partial(_out_proj_kernel, n_lat_tiles=nlt, router=router_w is not None),
        grid=(b, n_rows // tm),
        in_specs=in_specs, out_specs=out_specs, out_shape=out_shape,
        compiler_params=_cparams(("parallel", "parallel")),
        name="out_proj",
    )(*args)


def _ffn_kernel(x1_ref, h_ref, w1_ref, w3_ref, w2_ref, mod_ref, o_ref):
    h = h_ref[...]
    a = jnp.dot(h, w1_ref[...], preferred_element_type=jnp.float32)
    g = jnp.dot(h, w3_ref[...], preferred_element_type=jnp.float32)
    act = (a * jax.nn.sigmoid(a) * g).astype(jnp.bfloat16)
    y = jnp.dot(act, w2_ref[...], preferred_element_type=jnp.float32)
    o_ref[...] = x1_ref[...] + mod_ref[5:6, :] * y


def _dense_ffn(x1, h2, w1, w3, w2, modtab, n_lat):
    b, t, d = x1.shape
    tm = TOKEN_TILE
    nlt = n_lat // tm
    ff = w1.shape[1]
    tok = pl.BlockSpec((None, tm, d), lambda bi, i: (bi, i, 0))
    const = lambda shape: pl.BlockSpec(shape, lambda bi, i: (0, 0), pipeline_mode=pl.Buffered(1))
    return pl.pallas_call(
        _ffn_kernel,
        grid=(b, t // tm),
        in_specs=[tok, tok, const((d, ff)), const((d, ff)), const((ff, d)),
                  pl.BlockSpec((None, 6, d), lambda bi, i: (bi * 2 + (i >= nlt).astype(jnp.int32), 0, 0))],
        out_specs=tok,
        out_shape=jax.ShapeDtypeStruct((b, t, d), jnp.float32),
        compiler_params=_cparams(("parallel", "parallel")),
        name="dense_ffn",
    )(x1, h2, w1, w3, w2, modtab)


MOE_TILE = 512
MOE_FF_CHUNK = 896
TOP_K = 2


def _route_plan(eid, n_exp, n_tiles):
    experts = jnp.arange(n_exp, dtype=jnp.int32)
    counts = jnp.sum((eid[:, None] == experts[None, :]).astype(jnp.int32), axis=0)
    tiles = (counts + MOE_TILE - 1) // MOE_TILE
    tile_end = jnp.cumsum(tiles)
    n_used = tile_end[-1]
    j = jnp.arange(n_tiles, dtype=jnp.int32)
    owner = jnp.sum((j[:, None] >= tile_end[None, :]).astype(jnp.int32), axis=1)
    last_owner = jnp.sum((n_used - 1 >= tile_end).astype(jnp.int32))
    tile_expert = jnp.where(j < n_used, owner, last_owner).astype(jnp.int32)
    meta = jnp.concatenate([(tile_end - tiles) * MOE_TILE, counts, tile_end * MOE_TILE,
                            n_used[None]]).astype(jnp.int32)
    return meta, tile_expert


def _place_kernel(meta_ref, eid_ref, tos_ref, sot_ref, win_ref, cur_ref, *, n_exp, n_slots):
    i = pl.program_id(0)
    n_assign = eid_ref.shape[1]

    def clear(s, carry):
        tos_ref[s] = 0
        return carry

    @pl.when(i == 0)
    def _():
        for e in range(n_exp):
            cur_ref[e] = meta_ref[e]
            lax.fori_loop(meta_ref[e] + meta_ref[n_exp + e], meta_ref[2 * n_exp + e], clear, 0)
        lax.fori_loop(meta_ref[3 * n_exp] * MOE_TILE, n_slots, clear, 0)

    for e in range(n_exp):
        win_ref[i * n_exp + e] = cur_ref[e] // TOKEN_TILE

    def place(a, carry):
        e = eid_ref[0, a]
        p = cur_ref[e]
        cur_ref[e] = p + 1
        tos_ref[p] = i * (n_assign // TOP_K) + a // TOP_K
        sot_ref[0, a] = p
        return carry

    lax.fori_loop(0, n_assign, place, 0)


def _place(meta, eid, n_exp, n_slots):
    n_assign = TOKEN_TILE * TOP_K
    steps = eid.shape[0] // n_assign
    blocked = pl.BlockSpec((None, 1, n_assign), lambda i, meta: (i, 0, 0), memory_space=pltpu.SMEM)
    whole = pl.BlockSpec(memory_space=pltpu.SMEM)
    return pl.pallas_call(
        functools.partial(_place_kernel, n_exp=n_exp, n_slots=n_slots),
        grid_spec=pltpu.PrefetchScalarGridSpec(
            num_scalar_prefetch=1, grid=(steps,),
            in_specs=[blocked], out_specs=[whole, blocked, whole],
            scratch_shapes=[pltpu.SMEM((n_exp,), jnp.int32)]),
        out_shape=[jax.ShapeDtypeStruct((n_slots,), jnp.int32),
                   jax.ShapeDtypeStruct((steps, 1, n_assign), jnp.int32),
                   jax.ShapeDtypeStruct((steps * n_exp,), jnp.int32)],
        compiler_params=_cparams(("arbitrary",)),
        name="moe_place",
    )(meta, eid.reshape(steps, 1, n_assign))


def _experts_kernel(texp_ref, nused_ref, tos_ref, tab_ref, w1_ref, w3_ref, w2_ref, ys_ref,
                    xp_ref, xs_ref, acc_ref):
    j = pl.program_id(0)
    c = pl.program_id(1)
    used = j < nused_ref[0]
    last = c == pl.num_programs(1) - 1

    @pl.when(used & (c == 0))
    def _():
        def gather(s, carry):
            xp_ref[pl.ds(s, 1), :] = tab_ref[pl.ds(tos_ref[0, s], 1), :]
            return carry

        lax.fori_loop(0, MOE_TILE, gather, 0, unroll=8)
        xs_ref[...] = _unpack_halves(xp_ref[...]).astype(xs_ref.dtype)
        acc_ref[...] = jnp.zeros(acc_ref.shape, jnp.float32)

    @pl.when(used)
    def _():
        h = xs_ref[...]
        a = jnp.dot(h, w1_ref[...], preferred_element_type=jnp.float32)
        g = jnp.dot(h, w3_ref[...], preferred_element_type=jnp.float32)
        act = (a * jax.nn.sigmoid(a) * g).astype(jnp.bfloat16)
        acc_ref[...] += jnp.dot(act, w2_ref[...], preferred_element_type=jnp.float32)

    @pl.when(used & last)
    def _():
        ys_ref[...] = _pack_halves(acc_ref[...])

    @pl.when(jnp.logical_not(used) & last)
    def _():
        ys_ref[...] = _pack_halves(jnp.zeros(acc_ref.shape, jnp.float32))


def _experts(tile_expert, n_used, tos, table, w1, w3, w2):
    n_tiles = tos.shape[0]
    n_tok, half = table.shape
    d = 2 * half
    ff = w1.shape[2]
    fc = MOE_FF_CHUNK
    nc = ff // fc
    chunk = lambda j, c, te, nu: jnp.where(j < nu[0], c, nc - 1)
    return pl.pallas_call(
        _experts_kernel,
        grid_spec=pltpu.PrefetchScalarGridSpec(
            num_scalar_prefetch=2, grid=(n_tiles, nc),
            in_specs=[pl.BlockSpec((None, 1, MOE_TILE), lambda j, c, te, nu: (j, 0, 0),
                                   memory_space=pltpu.SMEM),
                      pl.BlockSpec((n_tok, half), lambda j, c, te, nu: (0, 0),
                                   pipeline_mode=pl.Buffered(1)),
                      pl.BlockSpec((None, d, fc), lambda j, c, te, nu: (te[j], 0, chunk(j, c, te, nu))),
                      pl.BlockSpec((None, d, fc), lambda j, c, te, nu: (te[j], 0, chunk(j, c, te, nu))),
                      pl.BlockSpec((None, fc, d), lambda j, c, te, nu: (te[j], chunk(j, c, te, nu), 0))],
            out_specs=pl.BlockSpec((MOE_TILE, half), lambda j, c, te, nu: (j, 0)),
            scratch_shapes=[pltpu.VMEM((MOE_TILE, half), jnp.uint32),
                            pltpu.VMEM((MOE_TILE, d), jnp.bfloat16),
                            pltpu.VMEM((MOE_TILE, d), jnp.float32)]),
        out_shape=jax.ShapeDtypeStruct((n_tiles * MOE_TILE, half), jnp.uint32),
        compiler_params=_cparams(("arbitrary", "arbitrary")),
        name="moe_experts",
    )(tile_expert, n_used, tos, table, w1, w3, w2)


def _combine_kernel(win_ref, x1_ref, route_ref, eid_ref, sot_ref, *rest, n_exp):
    ys_refs = rest[:2 * n_exp]
    mod_ref, gain_ref, o_ref, w_ref, r_ref = rest[2 * n_exp:]
    i = pl.program_id(0)
    tm = x1_ref.shape[0]
    for blk, ys_ref in enumerate(ys_refs):
        w_ref[pl.ds(blk * tm, tm), :] = ys_ref[...]

    def fetch(r, carry):
        for k in range(TOP_K):
            a = TOP_K * r + k
            e = eid_ref[0, a]
            loc = sot_ref[0, a] - win_ref[i * n_exp + e] * tm
            r_ref[k, pl.ds(r, 1), :] = w_ref[pl.ds(e * 2 * tm + loc, 1), :]
        return carry

    lax.fori_loop(0, tm, fetch, 0, unroll=4)
    route = route_ref[...]
    y = route[:, 2:3] * _unpack_halves(r_ref[0]) + route[:, 3:4] * _unpack_halves(r_ref[1])
    x2 = x1_ref[...] + mod_ref[5:6, :] * y
    o_ref[...] = _rms(x2) * gain_ref[...]


def _combine_final(win, x1, route, eid3, sot3, ys, modtab, final_gain, tiles_per_batch, n_exp):
    n_tok, d = x1.shape
    tm = TOKEN_TILE
    n_assign = tm * TOP_K
    ys_specs = [pl.BlockSpec((tm, d // 2), lambda i, win, e=e, jj=jj: (win[i * n_exp + e] + jj, 0))
                for e in range(n_exp) for jj in range(2)]
    smem_blk = pl.BlockSpec((None, 1, n_assign), lambda i, win: (i, 0, 0), memory_space=pltpu.SMEM)
    return pl.pallas_call(
        functools.partial(_combine_kernel, n_exp=n_exp),
        grid_spec=pltpu.PrefetchScalarGridSpec(
            num_scalar_prefetch=1, grid=(n_tok // tm,),
            in_specs=[pl.BlockSpec((tm, d), lambda i, win: (i, 0)),
                      pl.BlockSpec((tm, LANES), lambda i, win: (i, 0)),
                      smem_blk, smem_blk, *ys_specs,
                      pl.BlockSpec((None, 6, d), lambda i, win: ((i // tiles_per_batch) * 2, 0, 0)),
                      pl.BlockSpec((1, d), lambda i, win: (0, 0))],
            out_specs=pl.BlockSpec((tm, d), lambda i, win: (i, 0)),
            scratch_shapes=[pltpu.VMEM((2 * n_exp * tm, d // 2), jnp.uint32),
                            pltpu.VMEM((TOP_K, tm, d // 2), jnp.uint32)]),
        out_shape=jax.ShapeDtypeStruct((n_tok, d), jnp.float32),
        compiler_params=_cparams(("arbitrary",)),
        name="moe_combine",
    )(win, x1, route, eid3, sot3, *([ys] * (2 * n_exp)), modtab, final_gain.reshape(1, d))


def _moe_ffn_final(x1, h2p, route, w1, w3, w2, modtab, final_gain):
    b, n, d = x1.shape
    n_tok = b * n
    n_exp = w1.shape[0]
    n_tiles = n_tok * TOP_K // MOE_TILE + n_exp + 1
    eid = route[..., :TOP_K].astype(jnp.int32).reshape(n_tok * TOP_K)
    meta, tile_expert = _route_plan(eid, n_exp, n_tiles)
    tos, sot3, win = _place(meta, eid, n_exp, n_tiles * MOE_TILE)
    ys = _experts(tile_expert, meta[3 * n_exp:], tos.reshape(n_tiles, 1, MOE_TILE),
                  h2p, w1, w3, w2)
    eid3 = eid.reshape(n_tok // TOKEN_TILE, 1, TOKEN_TILE * TOP_K)
    out = _combine_final(win, x1.reshape(n_tok, d), route.reshape(n_tok, LANES), eid3, sot3, ys,
                         modtab, final_gain, n // TOKEN_TILE, n_exp)
    return out.reshape(b, n, d)


def _rope_tables(n_lat, n_ctx):
    t = jnp.arange(n_lat, dtype=jnp.int32)
    n_freq = HEAD_DIM // 4
    inv_freq = ROPE_BASE ** (-jnp.arange(n_freq, dtype=jnp.float32) / n_freq)
    ang = jnp.concatenate([(t // GRID_W).astype(jnp.float32)[:, None] * inv_freq,
                           (t % GRID_W).astype(jnp.float32)[:, None] * inv_freq], axis=-1)
    cos, sin = jnp.cos(ang), jnp.sin(ang)
    reps = LANES // HEAD_DIM
    cos_t = jnp.tile(jnp.concatenate([cos, cos], axis=-1), (1, reps))
    sin_t = jnp.tile(jnp.concatenate([-sin, sin], axis=-1), (1, reps))
    cos_t = jnp.concatenate([cos_t, jnp.ones((n_ctx, LANES), jnp.float32)], axis=0)
    sin_t = jnp.concatenate([sin_t, jnp.zeros((n_ctx, LANES), jnp.float32)], axis=0)
    return cos_t, sin_t


def kernel(x, c, ctx, c_ctx, ada_w, ada_b, w_in, w_out, na_rpb, diff_lambda, diff_subln, conv_w,
           ffn_w1, ffn_w3, ffn_w2, router_w, moe_w1, moe_w3, moe_w2, final_gain):
    b, n, d = x.shape
    n_ctx = ctx.shape[1]
    depth = w_in.shape[0]
    assert d == D_MODEL and n % TOKEN_TILE == 0 and n_ctx == TOKEN_TILE and b + 1 <= 8
    assert depth == 2, "layer 0 dense with a context stream, layer 1 routed and final"
    bf = jnp.bfloat16

    x_all = jnp.concatenate([x, ctx], axis=1)
    cvec = jnp.zeros((8, d), jnp.float32).at[:b].set(c).at[b].set(c_ctx)
    mod = _modulation(cvec, ada_w, ada_b).reshape(depth, 8, 6, d)
    cos_t, sin_t = _rope_tables(n, n_ctx)
    rows = n // GRID_W

    out = None
    for i in range(depth):
        lambda_init = 0.8 - 0.6 * math.exp(-0.3 * i)
        ctx_out = i < depth - 1
        modtab = jnp.stack([mod[i, :b], jnp.broadcast_to(mod[i, b], (b, 6, d))], axis=1).reshape(2 * b, 6, d)
        q, k, v, u, bg = _in_proj(x_all, modtab, cos_t, sin_t, w_in[i].astype(bf), n)
        bias = _na_bias_table(na_rpb[i], rows)
        na = _na_latent(q, k, v, bias, n)
        gain = diff_subln[i].reshape(1, 2 * HEAD_DIM)
        dif = _diff_attention(q, k, v, diff_lambda[i], gain, lambda_init,
                              q_rows=n, q_start=0, tq=512, k_rows=n + n_ctx, k_start=0, tk=768,
                              heads_per_step=4)
        if ctx_out:
            na_c = _na_context(q, k, v, n)
            dif_c = _diff_attention(q, k, v, diff_lambda[i], gain, lambda_init,
                                    q_rows=n_ctx, q_start=n, tq=n_ctx, k_rows=n_ctx, k_start=n, tk=n_ctx,
                                    heads_per_step=4)
            na = jnp.concatenate([na, na_c], axis=1)
            dif = jnp.concatenate([dif, dif_c], axis=1)
            x1, h2 = _out_proj(x_all, na, dif, u, bg, conv_w[i], w_out[i].astype(bf), modtab, n, n + n_ctx)
            m = i // 2
            x_all = _dense_ffn(x1, h2, ffn_w1[m].astype(bf), ffn_w3[m].astype(bf), ffn_w2[m].astype(bf),
                               modtab, n)
        else:
            m = i // 2
            rw = jnp.zeros((d, LANES), jnp.float32).at[:, :N_EXPERTS].set(router_w[m])
            x1, h2, gates = _out_proj(x_all, na, dif, u, bg, conv_w[i], w_out[i].astype(bf), modtab, n, n,
                                      router_w=rw)
            out = _moe_ffn_final(x1, h2, gates, moe_w1[m].astype(bf), moe_w3[m].astype(bf),
                                 moe_w2[m].astype(bf), modtab, final_gain)
    return out
```

```python
import functools
import math

import numpy as np
import jax
import jax.numpy as jnp
from jax import lax
from jax.experimental import pallas as pl
from jax.experimental.pallas import tpu as pltpu

D_MODEL = 1024
GRID_W = 64
HEAD_DIM = 64
NA_HEADS = 4
NA_WIDTH = NA_HEADS * HEAD_DIM
WIN_H = 8
WIN_W = 16
DIFF_HEADS = 4
DIFF_WIDTH = DIFF_HEADS * 2 * HEAD_DIM
CONV_WIDTH = 256
QKV_COLS = NA_WIDTH + DIFF_WIDTH
NA_COL_BLOCK = DIFF_WIDTH // NA_WIDTH
IN_COLS = 3 * QKV_COLS + 3 * CONV_WIDTH
ROPE_BASE = 10000.0
N_EXPERTS = 8
EPS = 1e-6
NEG_BIG = -1e30
LOG2E = math.log2(math.e)

LANES = 128
TOKEN_TILE = 256
NA_ROWS = 4
NA_KEY_ROWS = NA_ROWS + WIN_H
VMEM_LIMIT = 56 * 1024 * 1024


def _cparams(sem):
    return pltpu.CompilerParams(dimension_semantics=sem, vmem_limit_bytes=VMEM_LIMIT)


def _rms(x):
    return x * lax.rsqrt(jnp.mean(x * x, axis=-1, keepdims=True) + EPS)


def _pack_halves(x):
    half = x.shape[1] // 2
    return pltpu.pack_elementwise([x[:, :half], x[:, half:]], packed_dtype=jnp.bfloat16)


def _unpack_halves(p):
    parts = [pltpu.unpack_elementwise(p, index=i, packed_dtype=jnp.bfloat16, unpacked_dtype=jnp.float32)
             for i in range(2)]
    return jnp.concatenate(parts, axis=1)


def _mod_kernel(c_ref, w_ref, b_ref, o_ref):
    cv = c_ref[...]
    s = cv * jax.nn.sigmoid(cv)
    o_ref[...] = jnp.dot(s, w_ref[...], preferred_element_type=jnp.float32,
                         precision=lax.Precision.HIGHEST) + b_ref[...]


def _modulation(cvec, ada_w, ada_b):
    depth, d, cols = ada_w.shape
    cb = 1536
    return pl.pallas_call(
        _mod_kernel,
        grid=(depth, cols // cb),
        in_specs=[pl.BlockSpec((8, d), lambda i, j: (0, 0)),
                  pl.BlockSpec((None, d, cb), lambda i, j: (i, 0, j)),
                  pl.BlockSpec((None, 1, cb), lambda i, j: (i, 0, j))],
        out_specs=pl.BlockSpec((None, 8, cb), lambda i, j: (i, 0, j)),
        out_shape=jax.ShapeDtypeStruct((depth, 8, cols), jnp.float32),
        compiler_params=_cparams(("parallel", "parallel")),
        name="modulation",
    )(cvec, ada_w, ada_b.reshape(depth, 1, cols))


def _rope(z, cos, sin_signed):
    width = z.shape[1]
    reps = width // LANES
    c = jnp.concatenate([cos] * reps, axis=1)
    s = jnp.concatenate([sin_signed] * reps, axis=1)
    lane = lax.broadcasted_iota(jnp.int32, (1, width), 1)
    first_half = (lane % HEAD_DIM) < (HEAD_DIM // 2)
    swapped = jnp.where(first_half,
                        pltpu.roll(z, width - HEAD_DIM // 2, axis=1),
                        pltpu.roll(z, HEAD_DIM // 2, axis=1))
    return z * c + swapped * s


def _in_proj_kernel(x_ref, mod_ref, cos_ref, sin_ref, w_ref, q_ref, k_ref, v_ref, u_ref, bg_ref):
    x = x_ref[...]
    h = _rms(x) * (1.0 + mod_ref[1:2, :]) + mod_ref[0:1, :]
    p = jnp.dot(h.astype(jnp.bfloat16), w_ref[...], preferred_element_type=jnp.float32)
    cos = cos_ref[...]
    sin = sin_ref[...]
    scale = HEAD_DIM ** -0.5 * LOG2E
    q_ref[:, :DIFF_WIDTH] = (_rope(p[:, NA_WIDTH:QKV_COLS], cos, sin) * scale).astype(q_ref.dtype)
    q_ref[:, DIFF_WIDTH:] = (p[:, :NA_WIDTH] * scale).astype(q_ref.dtype)
    k_ref[:, :DIFF_WIDTH] = _rope(p[:, QKV_COLS + NA_WIDTH:2 * QKV_COLS], cos, sin).astype(k_ref.dtype)
    k_ref[:, DIFF_WIDTH:] = p[:, QKV_COLS:QKV_COLS + NA_WIDTH].astype(k_ref.dtype)
    o = 2 * QKV_COLS
    v_ref[:, :DIFF_WIDTH] = p[:, o + NA_WIDTH:o + QKV_COLS].astype(v_ref.dtype)
    v_ref[:, DIFF_WIDTH:] = p[:, o:o + NA_WIDTH].astype(v_ref.dtype)
    o = 3 * QKV_COLS
    xin = p[:, o:o + CONV_WIDTH]
    bg_ref[...] = p[:, o + CONV_WIDTH:o + 2 * CONV_WIDTH]
    u_ref[...] = p[:, o + 2 * CONV_WIDTH:o + 3 * CONV_WIDTH] * xin


def _in_proj(x_all, modtab, cos_t, sin_t, w_bf16, n_lat):
    b, t, d = x_all.shape
    tm = TOKEN_TILE
    nlt = n_lat // tm
    tok = lambda width: pl.BlockSpec((None, tm, width), lambda bi, i: (bi, i, 0))
    return pl.pallas_call(
        _in_proj_kernel,
        grid=(b, t // tm),
        in_specs=[tok(d),
                  pl.BlockSpec((None, 6, d), lambda bi, i: (bi * 2 + (i >= nlt).astype(jnp.int32), 0, 0)),
                  pl.BlockSpec((tm, LANES), lambda bi, i: (i, 0)),
                  pl.BlockSpec((tm, LANES), lambda bi, i: (i, 0)),
                  pl.BlockSpec((d, IN_COLS), lambda bi, i: (0, 0))],
        out_specs=[tok(QKV_COLS), tok(QKV_COLS), tok(QKV_COLS), tok(CONV_WIDTH), tok(CONV_WIDTH)],
        out_shape=[jax.ShapeDtypeStruct((b, t, QKV_COLS), jnp.bfloat16)] * 3
        + [jax.ShapeDtypeStruct((b, t, CONV_WIDTH), jnp.float32)] * 2,
        compiler_params=_cparams(("parallel", "parallel")),
        name="in_proj",
    )(x_all, modtab, cos_t, sin_t, w_bf16)


def _na_kernel(*refs, n_lat_blocks):
    q_ref = refs[0]
    k_refs = refs[1:2 + n_lat_blocks]
    v_refs = refs[2 + n_lat_blocks:3 + 2 * n_lat_blocks]
    if n_lat_blocks:
        bias_ref, o_ref = refs[3 + 2 * n_lat_blocks:]
    else:
        (o_ref,) = refs[3 + 2 * n_lat_blocks:]
    lane = lax.broadcasted_iota(jnp.int32, (1, LANES), 1)
    lat_keys = n_lat_blocks * TOKEN_TILE
    for pair in range(NA_HEADS // 2):
        cols = slice(pair * LANES, (pair + 1) * LANES)
        q = q_ref[:, cols]
        outs = []
        for hh in range(2):
            head = 2 * pair + hh
            in_head = (lane >= hh * HEAD_DIM) & (lane < (hh + 1) * HEAD_DIM)
            qm = jnp.where(in_head, q, jnp.zeros_like(q))
            s = jnp.concatenate(
                [lax.dot_general(qm, kr[:, cols], (((1,), (1,)), ((), ())),
                                 preferred_element_type=jnp.float32) for kr in k_refs], axis=1)
            if n_lat_blocks:
                s = jnp.concatenate([s[:, :lat_keys] + bias_ref[head], s[:, lat_keys:]], axis=1)
            m = jnp.max(s, axis=-1, keepdims=True)
            e = jnp.exp2(s - m)
            l = jnp.sum(e, axis=-1, keepdims=True)
            eb = e.astype(jnp.bfloat16)
            acc = None
            for j, vr in enumerate(v_refs):
                part = jnp.dot(eb[:, j * TOKEN_TILE:(j + 1) * TOKEN_TILE], vr[:, cols],
                               preferred_element_type=jnp.float32)
                acc = part if acc is None else acc + part
            outs.append(acc / l)
        o_ref[:, cols] = jnp.where(lane < HEAD_DIM, outs[0], outs[1]).astype(o_ref.dtype)


def _na_bias_table(rpb, rows):
    n_groups = rows // NA_ROWS
    heads = rpb.shape[0]
    padded = jnp.pad(rpb.astype(jnp.float32) * LOG2E, ((0, 0), (0, 0), (GRID_W, GRID_W)))
    toeplitz = jnp.stack([padded[:, :, GRID_W + WIN_W - 1 - qc:2 * GRID_W + WIN_W - 1 - qc]
                          for qc in range(GRID_W)], axis=2)
    qc = np.arange(GRID_W)[:, None]
    kc = np.arange(GRID_W)[None, :]
    col_start = np.clip(qc - WIN_W // 2, 0, GRID_W - WIN_W)
    col_valid = (kc >= col_start) & (kc < col_start + WIN_W)
    toeplitz = jnp.where(col_valid[None, None], toeplitz, NEG_BIG)
    masked = jnp.full((heads, GRID_W, GRID_W), NEG_BIG, jnp.float32)
    tables = []
    for g in (0, 1, n_groups - 1):
        ws = min(max(g - 1, 0), n_groups - 3) * NA_ROWS
        q_rows = []
        for qr in range(g * NA_ROWS, (g + 1) * NA_ROWS):
            row_start = min(max(qr - WIN_H // 2, 0), rows - WIN_H)
            blocks = [toeplitz[:, kr - qr + WIN_H - 1] if row_start <= kr < row_start + WIN_H else masked
                      for kr in range(ws, ws + NA_KEY_ROWS)]
            q_rows.append(jnp.concatenate(blocks, axis=-1))
        tables.append(jnp.concatenate(q_rows, axis=1))
    return jnp.stack(tables)


def _na_latent(q, k, v, bias, n_lat):
    b = q.shape[0]
    tm = TOKEN_TILE
    assert NA_ROWS * GRID_W == tm and NA_KEY_ROWS * GRID_W == 3 * tm
    ng = n_lat // tm
    ctx_blk = n_lat // tm

    def kv_spec(j):
        return pl.BlockSpec((None, tm, NA_WIDTH),
                            lambda bi, g: (bi, jnp.clip(g - 1, 0, ng - 3) + j, NA_COL_BLOCK))

    ctx_spec = pl.BlockSpec((None, tm, NA_WIDTH), lambda bi, g: (bi, ctx_blk, NA_COL_BLOCK))
    variant = lambda bi, g: ((g > 0).astype(jnp.int32) + (g == ng - 1).astype(jnp.int32), 0, 0, 0)
    return pl.pallas_call(
        functools.partial(_na_kernel, n_lat_blocks=3),
        grid=(b, ng),
        in_specs=[pl.BlockSpec((None, tm, NA_WIDTH), lambda bi, g: (bi, g, NA_COL_BLOCK)),
                  kv_spec(0), kv_spec(1), kv_spec(2), ctx_spec,
                  kv_spec(0), kv_spec(1), kv_spec(2), ctx_spec,
                  pl.BlockSpec((None, NA_HEADS, tm, 3 * tm), variant)],
        out_specs=pl.BlockSpec((None, tm, NA_WIDTH), lambda bi, g: (bi, g, 0)),
        out_shape=jax.ShapeDtypeStruct((b, n_lat, NA_WIDTH), jnp.bfloat16),
        compiler_params=_cparams(("parallel", "parallel")),
        name="na_latent",
    )(q, k, k, k, k, v, v, v, v, bias)


def _na_context(q, k, v, n_lat):
    b = q.shape[0]
    tm = TOKEN_TILE
    ctx_spec = pl.BlockSpec((None, tm, NA_WIDTH), lambda bi: (bi, n_lat // tm, NA_COL_BLOCK))
    return pl.pallas_call(
        functools.partial(_na_kernel, n_lat_blocks=0),
        grid=(b,),
        in_specs=[ctx_spec, ctx_spec, ctx_spec],
        out_specs=pl.BlockSpec((None, tm, NA_WIDTH), lambda bi: (bi, 0, 0)),
        out_shape=jax.ShapeDtypeStruct((b, tm, NA_WIDTH), jnp.bfloat16),
        compiler_params=_cparams(("parallel",)),
        name="na_context",
    )(q, k, v)


def _diff_kernel(q_ref, k_ref, v_ref, lam_ref, gain_ref, o_ref, m_ref, acc_ref, *, lambda_init):
    kk = pl.program_id(3)

    @pl.when(kk == 0)
    def _():
        m_ref[...] = jnp.full(m_ref.shape, NEG_BIG, jnp.float32)
        acc_ref[...] = jnp.zeros(acc_ref.shape, jnp.float32)

    lane = lax.broadcasted_iota(jnp.int32, (1, LANES), 1)
    reps = k_ref.shape[0] // LANES
    n_heads = q_ref.shape[1] // LANES
    for vh in range(n_heads):
        cols = slice(vh * LANES, (vh + 1) * LANES)
        q = q_ref[:, cols]
        k = k_ref[:, cols]
        v = v_ref[:, cols]
        v_ext = jnp.concatenate([v, jnp.ones_like(v)], axis=1)
        for hh in range(2):
            idx = 2 * vh + hh
            in_head = (lane >= hh * HEAD_DIM) & (lane < (hh + 1) * HEAD_DIM)
            qm = jnp.where(in_head, q, jnp.zeros_like(q))
            s = lax.dot_general(qm, k, (((1,), (1,)), ((), ())), preferred_element_type=jnp.float32)
            m_old = m_ref[idx]
            m_new = jnp.maximum(m_old, jnp.max(s, axis=-1, keepdims=True))
            alpha = jnp.exp2(m_old - m_new)
            p = jnp.exp2(s - jnp.concatenate([m_new] * reps, axis=1)).astype(jnp.bfloat16)
            pv = jnp.dot(p, v_ext, preferred_element_type=jnp.float32)
            acc_ref[idx] = jnp.concatenate([alpha, alpha], axis=1) * acc_ref[idx] + pv
            m_ref[idx] = m_new

    @pl.when(kk == pl.num_programs(3) - 1)
    def _():
        lp = lam_ref[...]
        lam = (jnp.exp(jnp.sum(lp[0:1] * lp[1:2], axis=-1, keepdims=True))
               - jnp.exp(jnp.sum(lp[2:3] * lp[3:4], axis=-1, keepdims=True)) + lambda_init)
        for vh in range(n_heads):
            a0 = acc_ref[2 * vh]
            a1 = acc_ref[2 * vh + 1]
            o = a0[:, :LANES] / a0[:, LANES:] - lam * (a1[:, :LANES] / a1[:, LANES:])
            o_ref[:, vh * LANES:(vh + 1) * LANES] = (
                _rms(o) * gain_ref[...] * (1.0 - lambda_init)).astype(o_ref.dtype)


def _diff_attention(q, k, v, lam_params, gain, lambda_init, *, q_rows, q_start, tq, k_rows, k_start, tk,
                    heads_per_step):
    b = q.shape[0]
    nq, nk = q_rows // tq, k_rows // tk
    qo, ko = q_start // tq, k_start // tk
    width = heads_per_step * LANES
    first = 0
    return pl.pallas_call(
        functools.partial(_diff_kernel, lambda_init=lambda_init),
        grid=(b, DIFF_HEADS // heads_per_step, nq, nk),
        in_specs=[pl.BlockSpec((None, tq, width), lambda bi, h, i, j: (bi, qo + i, first + h)),
                  pl.BlockSpec((None, tk, width), lambda bi, h, i, j: (bi, ko + j, first + h)),
                  pl.BlockSpec((None, tk, width), lambda bi, h, i, j: (bi, ko + j, first + h)),
                  pl.BlockSpec((4, HEAD_DIM), lambda bi, h, i, j: (0, 0)),
                  pl.BlockSpec((1, LANES), lambda bi, h, i, j: (0, 0))],
        out_specs=pl.BlockSpec((None, tq, width), lambda bi, h, i, j: (bi, i, h)),
        out_shape=jax.ShapeDtypeStruct((b, q_rows, DIFF_WIDTH), jnp.bfloat16),
        scratch_shapes=[pltpu.VMEM((2 * heads_per_step, tq, LANES), jnp.float32),
                        pltpu.VMEM((2 * heads_per_step, tq, 2 * LANES), jnp.float32)],
        compiler_params=_cparams(("parallel", "parallel", "parallel", "arbitrary")),
        name="diff_attention",
    )(q, k, v, lam_params, gain)


def _out_proj_kernel(x_ref, na_ref, dif_ref, u_ref, up_ref, un_ref, bg_ref, cw_ref, w_ref, mod_ref,
                     *rest, n_lat_tiles, router):
    if router:
        rw_ref, x1_ref, h2_ref, route_ref = rest
    else:
        x1_ref, h2_ref = rest
    i = pl.program_id(1)
    tm = x_ref.shape[0]
    first_of_seq = (i == 0) | (i == n_lat_tiles)
    last_of_seq = (i == n_lat_tiles - 1) | (i == n_lat_tiles)
    u = u_ref[...]
    up = jnp.where(first_of_seq, 0.0, up_ref[7:8, :])
    un = jnp.where(last_of_seq, 0.0, un_ref[0:1, :])
    row = lax.broadcasted_iota(jnp.int32, (tm, 1), 0)
    u_prev = jnp.where(row == 0, up, pltpu.roll(u, 1, axis=0))
    u_next = jnp.where(row == tm - 1, un, pltpu.roll(u, tm - 1, axis=0))
    conv = bg_ref[...] * (cw_ref[0:1, :] * u_prev + cw_ref[1:2, :] * u + cw_ref[2:3, :] * u_next)
    o1 = NA_WIDTH
    o2 = NA_WIDTH + DIFF_WIDTH
    mix = (jnp.dot(na_ref[...], w_ref[:o1, :], preferred_element_type=jnp.float32)
           + jnp.dot(dif_ref[...], w_ref[o1:o2, :], preferred_element_type=jnp.float32)
           + jnp.dot(conv.astype(jnp.bfloat16), w_ref[o2:, :], preferred_element_type=jnp.float32))
    x1 = x_ref[...] + mod_ref[2:3, :] * mix
    x1_ref[...] = x1
    h2 = _rms(x1) * (1.0 + mod_ref[4:5, :]) + mod_ref[3:4, :]
    if not router:
        h2_ref[...] = h2.astype(h2_ref.dtype)
    else:
        h2_ref[...] = _pack_halves(h2)
        h_hi = h2.astype(jnp.bfloat16)
        h_lo = (h2 - h_hi.astype(jnp.float32)).astype(jnp.bfloat16)
        logits = (jnp.dot(h_hi, rw_ref[0], preferred_element_type=jnp.float32)
                  + (jnp.dot(h_hi, rw_ref[1], preferred_element_type=jnp.float32)
                     + jnp.dot(h_lo, rw_ref[0], preferred_element_type=jnp.float32)))
        lane = lax.broadcasted_iota(jnp.int32, logits.shape, 1)
        logits = jnp.where(lane < N_EXPERTS, logits, NEG_BIG)
        m1 = jnp.max(logits, axis=-1, keepdims=True)
        i1 = jnp.min(jnp.where(logits == m1, lane, LANES), axis=-1, keepdims=True)
        rest_l = jnp.where(lane == i1, NEG_BIG, logits)
        m2 = jnp.max(rest_l, axis=-1, keepdims=True)
        i2 = jnp.min(jnp.where(rest_l == m2, lane, LANES), axis=-1, keepdims=True)
        e2 = jnp.exp(m2 - m1)
        w1 = 1.0 / (1.0 + e2)
        route_ref[...] = jnp.where(
            lane == 0, i1.astype(jnp.float32),
            jnp.where(lane == 1, i2.astype(jnp.float32),
                      jnp.where(lane == 2, w1, jnp.where(lane == 3, e2 * w1, 0.0))))


def _out_proj(x_all, na, dif, u, bg, conv_w, w_bf16, modtab, n_lat, n_rows, router_w=None):
    b, t, d = x_all.shape
    tm = TOKEN_TILE
    nlt = n_lat // tm
    sub = 8
    n_sub = t // sub
    tok = lambda width: pl.BlockSpec((None, tm, width), lambda bi, i: (bi, i, 0))
    in_specs = [tok(d), tok(NA_WIDTH), tok(DIFF_WIDTH), tok(CONV_WIDTH),
                pl.BlockSpec((None, sub, CONV_WIDTH),
                             lambda bi, i: (bi, jnp.maximum(i * (tm // sub) - 1, 0), 0)),
                pl.BlockSpec((None, sub, CONV_WIDTH),
                             lambda bi, i: (bi, jnp.minimum((i + 1) * (tm // sub), n_sub - 1), 0)),
                tok(CONV_WIDTH),
                pl.BlockSpec((3, CONV_WIDTH), lambda bi, i: (0, 0)),
                pl.BlockSpec((d, d), lambda bi, i: (0, 0)),
                pl.BlockSpec((None, 6, d), lambda bi, i: (bi * 2 + (i >= nlt).astype(jnp.int32), 0, 0))]
    args = [x_all, na, dif, u, u, u, bg, conv_w, w_bf16, modtab]
    if router_w is None:
        out_specs = [tok(d), tok(d)]
        out_shape = [jax.ShapeDtypeStruct((b, n_rows, d), jnp.float32),
                     jax.ShapeDtypeStruct((b, n_rows, d), jnp.bfloat16)]
    else:
        in_specs.append(pl.BlockSpec((2, d, LANES), lambda bi, i: (0, 0, 0)))
        steps = n_rows // tm
        out_specs = [tok(d), pl.BlockSpec((tm, d // 2), lambda bi, i: (bi * steps + i, 0)), tok(LANES)]
        out_shape = [jax.ShapeDtypeStruct((b, n_rows, d), jnp.float32),
                     jax.ShapeDtypeStruct((b * n_rows, d // 2), jnp.uint32),
                     jax.ShapeDtypeStruct((b, n_rows, LANES), jnp.float32)]
        args.append(router_w)
    return pl.pallas_call(
        functools.partial(_out_proj_kernel, n_lat_tiles=nlt, router=router_w is not None),
        grid=(b, n_rows // tm),
        in_specs=in_specs, out_specs=out_specs, out_shape=out_shape,
        compiler_params=_cparams(("parallel", "parallel")),
        name="out_proj",
    )(*args)


def _ffn_kernel(x1_ref, h_ref, w1_ref, w3_ref, w2_ref, mod_ref, o_ref):
    h = h_ref[...]
    a = jnp.dot(h, w1_ref[...], preferred_element_type=jnp.float32)
    g = jnp.dot(h, w3_ref[...], preferred_element_type=jnp.float32)
    act = (a * jax.nn.sigmoid(a) * g).astype(jnp.bfloat16)
    y = jnp.dot(act, w2_ref[...], preferred_element_type=jnp.float32)
    o_ref[...] = x1_ref[...] + mod_ref[5:6, :] * y


def _dense_ffn(x1, h2, w1, w3, w2, modtab, n_lat):
    b, t, d = x1.shape
    tm = TOKEN_TILE
    nlt = n_lat // tm
    ff = w1.shape[1]
    tok = pl.BlockSpec((None, tm, d), lambda bi, i: (bi, i, 0))
    const = lambda shape: pl.BlockSpec(shape, lambda bi, i: (0, 0), pipeline_mode=pl.Buffered(1))
    return pl.pallas_call(
        _ffn_kernel,
        grid=(b, t // tm),
        in_specs=[tok, tok, const((d, ff)), const((d, ff)), const((ff, d)),
                  pl.BlockSpec((None, 6, d), lambda bi, i: (bi * 2 + (i >= nlt).astype(jnp.int32), 0, 0))],
        out_specs=tok,
        out_shape=jax.ShapeDtypeStruct((b, t, d), jnp.float32),
        compiler_params=_cparams(("parallel", "parallel")),
        name="dense_ffn",
    )(x1, h2, w1, w3, w2, modtab)


MOE_TILE = 512
MOE_FF_CHUNK = 896
TOP_K = 2


def _route_plan(eid, n_exp, n_tiles):
    experts = jnp.arange(n_exp, dtype=jnp.int32)
    counts = jnp.sum((eid[:, None] == experts[None, :]).astype(jnp.int32), axis=0)
    tiles = (counts + MOE_TILE - 1) // MOE_TILE
    tile_end = jnp.cumsum(tiles)
    n_used = tile_end[-1]
    j = jnp.arange(n_tiles, dtype=jnp.int32)
    owner = jnp.sum((j[:, None] >= tile_end[None, :]).astype(jnp.int32), axis=1)
    last_owner = jnp.sum((n_used - 1 >= tile_end).astype(jnp.int32))
    tile_expert = jnp.where(j < n_used, owner, last_owner).astype(jnp.int32)
    meta = jnp.concatenate([(tile_end - tiles) * MOE_TILE, counts, tile_end * MOE_TILE,
                            n_used[None]]).astype(jnp.int32)
    return meta, tile_expert


def _slot_kernel(eid_ref, start_ref, tri_ref, sot_ref, wrow_ref, win_ref, next_ref):
    @pl.when(pl.program_id(0) == 0)
    def _():
        next_ref[...] = start_ref[...]

    e_row = eid_ref[...]
    expert = lax.broadcasted_iota(jnp.int32, (next_ref.shape[0], e_row.shape[1]), 0)
    hit = e_row == expert
    onehot = jnp.where(hit, 1.0, 0.0)
    before = jnp.dot(onehot.astype(jnp.bfloat16), tri_ref[...], preferred_element_type=jnp.float32)
    nxt = next_ref[:, 0:1]
    blk = jnp.floor(nxt * (1.0 / TOKEN_TILE))
    slot = before + nxt
    sot_ref[...] = jnp.sum(jnp.where(hit, slot, 0.0), axis=0, keepdims=True).astype(jnp.int32)
    wrow = expert.astype(jnp.float32) * (2 * TOKEN_TILE) + slot - blk * TOKEN_TILE
    wrow_ref[...] = jnp.sum(jnp.where(hit, wrow, 0.0), axis=0, keepdims=True).astype(jnp.int32)
    win_ref[...] = jnp.broadcast_to(blk, win_ref.shape).astype(jnp.int32)
    next_ref[...] = next_ref[...] + jnp.sum(onehot, axis=1, keepdims=True)


def _slots(eid, start, n_exp):
    n_assign = TOKEN_TILE * TOP_K
    steps = eid.shape[0] // n_assign
    assert n_exp == 8, "experts are laid along the eight sublanes"
    tri = jnp.asarray(np.triu(np.ones((n_assign, n_assign), np.float32), k=1), jnp.bfloat16)
    row = pl.BlockSpec((None, 1, n_assign), lambda i: (i, 0, 0))
    return pl.pallas_call(
        _slot_kernel,
        grid=(steps,),
        in_specs=[row, pl.BlockSpec((n_exp, LANES), lambda i: (0, 0)),
                  pl.BlockSpec((n_assign, n_assign), lambda i: (0, 0))],
        out_specs=[row, row, pl.BlockSpec((None, n_exp, LANES), lambda i: (i, 0, 0))],
        out_shape=[jax.ShapeDtypeStruct((steps, 1, n_assign), jnp.int32),
                   jax.ShapeDtypeStruct((steps, 1, n_assign), jnp.int32),
                   jax.ShapeDtypeStruct((steps, n_exp, LANES), jnp.int32)],
        scratch_shapes=[pltpu.VMEM((n_exp, LANES), jnp.float32)],
        compiler_params=_cparams(("arbitrary",)),
        name="moe_slots",
    )(eid.reshape(steps, 1, n_assign),
      jnp.broadcast_to(start.astype(jnp.float32)[:, None], (n_exp, LANES)), tri)


INVERT_CHUNK = 4096


def _invert_kernel(meta_ref, sot_ref, tos_ref, *, n_exp, n_slots):
    i = pl.program_id(0)
    n_assign = sot_ref.shape[1]

    def clear(s, carry):
        tos_ref[s] = 0
        return carry

    @pl.when(i == 0)
    def _():
        for e in range(n_exp):
            lax.fori_loop(meta_ref[e] + meta_ref[n_exp + e], meta_ref[2 * n_exp + e], clear, 0)
        lax.fori_loop(meta_ref[3 * n_exp] * MOE_TILE, n_slots, clear, 0)

    first = i * (n_assign // TOP_K)

    def put(t, carry):
        for k in range(TOP_K):
            tos_ref[sot_ref[0, TOP_K * t + k]] = first + t
        return carry

    lax.fori_loop(0, n_assign // TOP_K, put, 0, unroll=8)


def _invert(meta, sot, n_exp, n_slots):
    steps = sot.size // INVERT_CHUNK
    blocked = pl.BlockSpec((None, 1, INVERT_CHUNK), lambda i, meta: (i, 0, 0), memory_space=pltpu.SMEM)
    return pl.pallas_call(
        functools.partial(_invert_kernel, n_exp=n_exp, n_slots=n_slots),
        grid_spec=pltpu.PrefetchScalarGridSpec(
            num_scalar_prefetch=1, grid=(steps,),
            in_specs=[blocked], out_specs=pl.BlockSpec(memory_space=pltpu.SMEM)),
        out_shape=jax.ShapeDtypeStruct((n_slots,), jnp.int32),
        compiler_params=_cparams(("arbitrary",)),
        name="moe_invert",
    )(meta, sot.reshape(steps, 1, INVERT_CHUNK))


def _experts_kernel(texp_ref, nused_ref, tos_ref, tab_ref, w1_ref, w3_ref, w2_ref, ys_ref,
                    xp_ref, xs_ref, acc_ref):
    j = pl.program_id(0)
    c = pl.program_id(1)
    used = j < nused_ref[0]
    last = c == pl.num_programs(1) - 1

    @pl.when(used & (c == 0))
    def _():
        def gather(s, carry):
            xp_ref[pl.ds(s, 1), :] = tab_ref[pl.ds(tos_ref[0, s], 1), :]
            return carry

        lax.fori_loop(0, MOE_TILE, gather, 0, unroll=8)
        xs_ref[...] = _unpack_halves(xp_ref[...]).astype(xs_ref.dtype)
        acc_ref[...] = jnp.zeros(acc_ref.shape, jnp.float32)

    @pl.when(used)
    def _():
        h = xs_ref[...]
        a = jnp.dot(h, w1_ref[...], preferred_element_type=jnp.float32)
        g = jnp.dot(h, w3_ref[...], preferred_element_type=jnp.float32)
        act = (a * jax.nn.sigmoid(a) * g).astype(jnp.bfloat16)
        acc_ref[...] += jnp.dot(act, w2_ref[...], preferred_element_type=jnp.float32)

    @pl.when(used & last)
    def _():
        ys_ref[...] = _pack_halves(acc_ref[...])

    @pl.when(jnp.logical_not(used) & last)
    def _():
        ys_ref[...] = _pack_halves(jnp.zeros(acc_ref.shape, jnp.float32))


def _experts(tile_expert, n_used, tos, table, w1, w3, w2):
    n_tiles = tos.shape[0]
    n_tok, half = table.shape
    d = 2 * half
    ff = w1.shape[2]
    fc = MOE_FF_CHUNK
    nc = ff // fc
    chunk = lambda j, c, te, nu: jnp.where(j < nu[0], c, nc - 1)
    return pl.pallas_call(
        _experts_kernel,
        grid_spec=pltpu.PrefetchScalarGridSpec(
            num_scalar_prefetch=2, grid=(n_tiles, nc),
            in_specs=[pl.BlockSpec((None, 1, MOE_TILE), lambda j, c, te, nu: (j, 0, 0),
                                   memory_space=pltpu.SMEM),
                      pl.BlockSpec((n_tok, half), lambda j, c, te, nu: (0, 0),
                                   pipeline_mode=pl.Buffered(1)),
                      pl.BlockSpec((None, d, fc), lambda j, c, te, nu: (te[j], 0, chunk(j, c, te, nu))),
                      pl.BlockSpec((None, d, fc), lambda j, c, te, nu: (te[j], 0, chunk(j, c, te, nu))),
                      pl.BlockSpec((None, fc, d), lambda j, c, te, nu: (te[j], chunk(j, c, te, nu), 0))],
            out_specs=pl.BlockSpec((MOE_TILE, half), lambda j, c, te, nu: (j, 0)),
            scratch_shapes=[pltpu.VMEM((MOE_TILE, half), jnp.uint32),
                            pltpu.VMEM((MOE_TILE, d), jnp.bfloat16),
                            pltpu.VMEM((MOE_TILE, d), jnp.float32)]),
        out_shape=jax.ShapeDtypeStruct((n_tiles * MOE_TILE, half), jnp.uint32),
        compiler_params=_cparams(("arbitrary", "arbitrary")),
        name="moe_experts",
    )(tile_expert, n_used, tos, table, w1, w3, w2)


def _combine_kernel(win_ref, x1_ref, route_ref, wrow_ref, *rest, n_exp):
    ys_refs = rest[:2 * n_exp]
    mod_ref, gain_ref, o_ref, w_ref, r_ref = rest[2 * n_exp:]
    tm = x1_ref.shape[0]
    for blk, ys_ref in enumerate(ys_refs):
        w_ref[pl.ds(blk * tm, tm), :] = ys_ref[...]

    def fetch(r, carry):
        for k in range(TOP_K):
            r_ref[k, pl.ds(r, 1), :] = w_ref[pl.ds(wrow_ref[0, TOP_K * r + k], 1), :]
        return carry

    lax.fori_loop(0, tm, fetch, 0, unroll=8)
    route = route_ref[...]
    y = route[:, 2:3] * _unpack_halves(r_ref[0]) + route[:, 3:4] * _unpack_halves(r_ref[1])
    x2 = x1_ref[...] + mod_ref[5:6, :] * y
    o_ref[...] = _rms(x2) * gain_ref[...]


def _combine_final(win, x1, route, wrow3, ys, modtab, final_gain, tiles_per_batch, n_exp):
    n_tok, d = x1.shape
    tm = TOKEN_TILE
    n_assign = tm * TOP_K
    ys_specs = [pl.BlockSpec((tm, d // 2), lambda i, win, e=e, jj=jj: (win[i * n_exp + e] + jj, 0))
                for e in range(n_exp) for jj in range(2)]
    smem_blk = pl.BlockSpec((None, 1, n_assign), lambda i, win: (i, 0, 0), memory_space=pltpu.SMEM)
    return pl.pallas_call(
        functools.partial(_combine_kernel, n_exp=n_exp),
        grid_spec=pltpu.PrefetchScalarGridSpec(
            num_scalar_prefetch=1, grid=(n_tok // tm,),
            in_specs=[pl.BlockSpec((tm, d), lambda i, win: (i, 0)),
                      pl.BlockSpec((tm, LANES), lambda i, win: (i, 0)),
                      smem_blk, *ys_specs,
                      pl.BlockSpec((None, 6, d), lambda i, win: ((i // tiles_per_batch) * 2, 0, 0)),
                      pl.BlockSpec((1, d), lambda i, win: (0, 0))],
            out_specs=pl.BlockSpec((tm, d), lambda i, win: (i, 0)),
            scratch_shapes=[pltpu.VMEM((2 * n_exp * tm, d // 2), jnp.uint32),
                            pltpu.VMEM((TOP_K, tm, d // 2), jnp.uint32)]),
        out_shape=jax.ShapeDtypeStruct((n_tok, d), jnp.float32),
        compiler_params=_cparams(("arbitrary",)),
        name="moe_combine",
    )(win, x1, route, wrow3, *([ys] * (2 * n_exp)), modtab, final_gain.reshape(1, d))


def _moe_ffn_final(x1, h2p, route, w1, w3, w2, modtab, final_gain):
    b, n, d = x1.shape
    n_tok = b * n
    n_exp = w1.shape[0]
    n_tiles = n_tok * TOP_K // MOE_TILE + n_exp + 1
    eid = route[..., :TOP_K].astype(jnp.int32).reshape(n_tok * TOP_K)
    meta, tile_expert = _route_plan(eid, n_exp, n_tiles)
    sot3, wrow3, win = _slots(eid, meta[:n_exp], n_exp)
    tos = _invert(meta, sot3, n_exp, n_tiles * MOE_TILE)
    ys = _experts(tile_expert, meta[3 * n_exp:], tos.reshape(n_tiles, 1, MOE_TILE),
                  h2p, w1, w3, w2)
    out = _combine_final(win[:, :, 0].reshape(-1), x1.reshape(n_tok, d), route.reshape(n_tok, LANES),
                         wrow3, ys, modtab, final_gain, n // TOKEN_TILE, n_exp)
    return out.reshape(b, n, d)


def _rope_tables(n_lat, n_ctx):
    t = jnp.arange(n_lat, dtype=jnp.int32)
    n_freq = HEAD_DIM // 4
    inv_freq = ROPE_BASE ** (-jnp.arange(n_freq, dtype=jnp.float32) / n_freq)
    ang = jnp.concatenate([(t // GRID_W).astype(jnp.float32)[:, None] * inv_freq,
                           (t % GRID_W).astype(jnp.float32)[:, None] * inv_freq], axis=-1)
    cos, sin = jnp.cos(ang), jnp.sin(ang)
    reps = LANES // HEAD_DIM
    cos_t = jnp.tile(jnp.concatenate([cos, cos], axis=-1), (1, reps))
    sin_t = jnp.tile(jnp.concatenate([-sin, sin], axis=-1), (1, reps))
    cos_t = jnp.concatenate([cos_t, jnp.ones((n_ctx, LANES), jnp.float32)], axis=0)
    sin_t = jnp.concatenate([sin_t, jnp.zeros((n_ctx, LANES), jnp.float32)], axis=0)
    return cos_t, sin_t


def kernel(x, c, ctx, c_ctx, ada_w, ada_b, w_in, w_out, na_rpb, diff_lambda, diff_subln, conv_w,
           ffn_w1, ffn_w3, ffn_w2, router_w, moe_w1, moe_w3, moe_w2, final_gain):
    b, n, d = x.shape
    n_ctx = ctx.shape[1]
    depth = w_in.shape[0]
    assert d == D_MODEL and n % TOKEN_TILE == 0 and n_ctx == TOKEN_TILE and b + 1 <= 8
    assert depth == 2, "layer 0 dense with a context stream, layer 1 routed and final"
    bf = jnp.bfloat16

    x_all = jnp.concatenate([x, ctx], axis=1)
    cvec = jnp.zeros((8, d), jnp.float32).at[:b].set(c).at[b].set(c_ctx)
    mod = _modulation(cvec, ada_w, ada_b).reshape(depth, 8, 6, d)
    cos_t, sin_t = _rope_tables(n, n_ctx)
    rows = n // GRID_W

    out = None
    for i in range(depth):
        lambda_init = 0.8 - 0.6 * math.exp(-0.3 * i)
        ctx_out = i < depth - 1
        modtab = jnp.stack([mod[i, :b], jnp.broadcast_to(mod[i, b], (b, 6, d))], axis=1).reshape(2 * b, 6, d)
        q, k, v, u, bg = _in_proj(x_all, modtab, cos_t, sin_t, w_in[i].astype(bf), n)
        bias = _na_bias_table(na_rpb[i], rows)
        na = _na_latent(q, k, v, bias, n)
        gain = diff_subln[i].reshape(1, 2 * HEAD_DIM)
        dif = _diff_attention(q, k, v, diff_lambda[i], gain, lambda_init,
                              q_rows=n, q_start=0, tq=1024, k_rows=n + n_ctx, k_start=0, tk=768,
                              heads_per_step=4)
        if ctx_out:
            na_c = _na_context(q, k, v, n)
            dif_c = _diff_attention(q, k, v, diff_lambda[i], gain, lambda_init,
                                    q_rows=n_ctx, q_start=n, tq=n_ctx, k_rows=n_ctx, k_start=n, tk=n_ctx,
                                    heads_per_step=4)
            na = jnp.concatenate([na, na_c], axis=1)
            dif = jnp.concatenate([dif, dif_c], axis=1)
            x1, h2 = _out_proj(x_all, na, dif, u, bg, conv_w[i], w_out[i].astype(bf), modtab, n, n + n_ctx)
            m = i // 2
            x_all = _dense_ffn(x1, h2, ffn_w1[m].astype(bf), ffn_w3[m].astype(bf), ffn_w2[m].astype(bf),
                               modtab, n)
        else:
            m = i // 2
            rw = jnp.zeros((d, LANES), jnp.float32).at[:, :N_EXPERTS].set(router_w[m])
            rw_hi = rw.astype(bf)
            rw = jnp.stack([rw_hi, (rw - rw_hi.astype(jnp.float32)).astype(bf)])
            x1, h2, gates = _out_proj(x_all, na, dif, u, bg, conv_w[i], w_out[i].astype(bf), modtab, n, n,
                                      router_w=rw)
            out = _moe_ffn_final(x1, h2, gates, moe_w1[m].astype(bf), moe_w3[m].astype(bf),
                                 moe_w2[m].astype(bf), modtab, final_gain)
    return out
```

```python
import functools
import math

import numpy as np
import jax
import jax.numpy as jnp
from jax import lax
from jax.experimental import pallas as pl
from jax.experimental.pallas import tpu as pltpu

D_MODEL = 1024
GRID_W = 64
HEAD_DIM = 64
NA_HEADS = 4
NA_WIDTH = NA_HEADS * HEAD_DIM
WIN_H = 8
WIN_W = 16
DIFF_HEADS = 4
DIFF_WIDTH = DIFF_HEADS * 2 * HEAD_DIM
CONV_WIDTH = 256
QKV_COLS = NA_WIDTH + DIFF_WIDTH
NA_COL_BLOCK = DIFF_WIDTH // NA_WIDTH
IN_COLS = 3 * QKV_COLS + 3 * CONV_WIDTH
ROPE_BASE = 10000.0
N_EXPERTS = 8
EPS = 1e-6
NEG_BIG = -1e30
LOG2E = math.log2(math.e)

LANES = 128
TOKEN_TILE = 256
NA_ROWS = 4
NA_KEY_ROWS = NA_ROWS + WIN_H
VMEM_LIMIT = 56 * 1024 * 1024


def _cparams(sem):
    return pltpu.CompilerParams(dimension_semantics=sem, vmem_limit_bytes=VMEM_LIMIT)


def _rms(x):
    return x * lax.rsqrt(jnp.mean(x * x, axis=-1, keepdims=True) + EPS)


def _pack_halves(x):
    half = x.shape[1] // 2
    return pltpu.pack_elementwise([x[:, :half], x[:, half:]], packed_dtype=jnp.bfloat16)


def _unpack_halves(p):
    parts = [pltpu.unpack_elementwise(p, index=i, packed_dtype=jnp.bfloat16, unpacked_dtype=jnp.float32)
             for i in range(2)]
    return jnp.concatenate(parts, axis=1)


def _mod_kernel(c_ref, w_ref, b_ref, o_ref):
    cv = c_ref[...]
    s = cv * jax.nn.sigmoid(cv)
    o_ref[...] = jnp.dot(s, w_ref[...], preferred_element_type=jnp.float32,
                         precision=lax.Precision.HIGHEST) + b_ref[...]


def _modulation(cvec, ada_w, ada_b):
    depth, d, cols = ada_w.shape
    cb = 1536
    return pl.pallas_call(
        _mod_kernel,
        grid=(depth, cols // cb),
        in_specs=[pl.BlockSpec((8, d), lambda i, j: (0, 0)),
                  pl.BlockSpec((None, d, cb), lambda i, j: (i, 0, j)),
                  pl.BlockSpec((None, 1, cb), lambda i, j: (i, 0, j))],
        out_specs=pl.BlockSpec((None, 8, cb), lambda i, j: (i, 0, j)),
        out_shape=jax.ShapeDtypeStruct((depth, 8, cols), jnp.float32),
        compiler_params=_cparams(("parallel", "parallel")),
        name="modulation",
    )(cvec, ada_w, ada_b.reshape(depth, 1, cols))


def _rope(z, cos, sin_signed):
    width = z.shape[1]
    reps = width // LANES
    c = jnp.concatenate([cos] * reps, axis=1)
    s = jnp.concatenate([sin_signed] * reps, axis=1)
    lane = lax.broadcasted_iota(jnp.int32, (1, width), 1)
    first_half = (lane % HEAD_DIM) < (HEAD_DIM // 2)
    swapped = jnp.where(first_half,
                        pltpu.roll(z, width - HEAD_DIM // 2, axis=1),
                        pltpu.roll(z, HEAD_DIM // 2, axis=1))
    return z * c + swapped * s


def _stream_specs(lat, n_lat, ctx_block):
    tm = TOKEN_TILE
    nlt = n_lat // tm
    d = lat.shape[-1]
    return [pl.BlockSpec((None, tm, d), lambda bi, i: (bi, jnp.minimum(i, nlt - 1), 0)),
            pl.BlockSpec((None, tm, d), lambda bi, i: (bi, ctx_block, 0))]


def _in_proj_kernel(x_ref, c_ref, mod_ref, cos_ref, sin_ref, w_ref, q_ref, k_ref, v_ref, u_ref, bg_ref,
                    *, n_lat_tiles):
    x = jnp.where(pl.program_id(1) >= n_lat_tiles, c_ref[...], x_ref[...])
    h = _rms(x) * (1.0 + mod_ref[1:2, :]) + mod_ref[0:1, :]
    p = jnp.dot(h.astype(jnp.bfloat16), w_ref[...], preferred_element_type=jnp.float32)
    cos = cos_ref[...]
    sin = sin_ref[...]
    scale = HEAD_DIM ** -0.5 * LOG2E
    q_ref[:, :DIFF_WIDTH] = (_rope(p[:, NA_WIDTH:QKV_COLS], cos, sin) * scale).astype(q_ref.dtype)
    q_ref[:, DIFF_WIDTH:] = (p[:, :NA_WIDTH] * scale).astype(q_ref.dtype)
    k_ref[:, :DIFF_WIDTH] = _rope(p[:, QKV_COLS + NA_WIDTH:2 * QKV_COLS], cos, sin).astype(k_ref.dtype)
    k_ref[:, DIFF_WIDTH:] = p[:, QKV_COLS:QKV_COLS + NA_WIDTH].astype(k_ref.dtype)
    o = 2 * QKV_COLS
    v_ref[:, :DIFF_WIDTH] = p[:, o + NA_WIDTH:o + QKV_COLS].astype(v_ref.dtype)
    v_ref[:, DIFF_WIDTH:] = p[:, o:o + NA_WIDTH].astype(v_ref.dtype)
    o = 3 * QKV_COLS
    xin = p[:, o:o + CONV_WIDTH]
    bg_ref[...] = p[:, o + CONV_WIDTH:o + 2 * CONV_WIDTH]
    u_ref[...] = p[:, o + 2 * CONV_WIDTH:o + 3 * CONV_WIDTH] * xin


def _in_proj(lat, ctx, ctx_block, modtab, cos_t, sin_t, w_bf16, n_lat):
    b, _, d = lat.shape
    tm = TOKEN_TILE
    nlt = n_lat // tm
    t = n_lat + tm
    tok = lambda width: pl.BlockSpec((None, tm, width), lambda bi, i: (bi, i, 0))
    return pl.pallas_call(
        functools.partial(_in_proj_kernel, n_lat_tiles=nlt),
        grid=(b, t // tm),
        in_specs=[*_stream_specs(lat, n_lat, ctx_block),
                  pl.BlockSpec((None, 6, d), lambda bi, i: (bi * 2 + (i >= nlt).astype(jnp.int32), 0, 0)),
                  pl.BlockSpec((tm, LANES), lambda bi, i: (i, 0)),
                  pl.BlockSpec((tm, LANES), lambda bi, i: (i, 0)),
                  pl.BlockSpec((d, IN_COLS), lambda bi, i: (0, 0))],
        out_specs=[tok(QKV_COLS), tok(QKV_COLS), tok(QKV_COLS), tok(CONV_WIDTH), tok(CONV_WIDTH)],
        out_shape=[jax.ShapeDtypeStruct((b, t, QKV_COLS), jnp.bfloat16)] * 3
        + [jax.ShapeDtypeStruct((b, t, CONV_WIDTH), jnp.float32)] * 2,
        compiler_params=_cparams(("parallel", "parallel")),
        name="in_proj",
    )(lat, ctx, modtab, cos_t, sin_t, w_bf16)


def _na_kernel(*refs, n_lat_blocks):
    q_ref = refs[0]
    k_refs = refs[1:2 + n_lat_blocks]
    v_refs = refs[2 + n_lat_blocks:3 + 2 * n_lat_blocks]
    if n_lat_blocks:
        bias_ref, o_ref = refs[3 + 2 * n_lat_blocks:]
    else:
        (o_ref,) = refs[3 + 2 * n_lat_blocks:]
    lane = lax.broadcasted_iota(jnp.int32, (1, LANES), 1)
    lat_keys = n_lat_blocks * TOKEN_TILE
    for pair in range(NA_HEADS // 2):
        cols = slice(pair * LANES, (pair + 1) * LANES)
        q = q_ref[:, cols]
        outs = []
        for hh in range(2):
            head = 2 * pair + hh
            in_head = (lane >= hh * HEAD_DIM) & (lane < (hh + 1) * HEAD_DIM)
            qm = jnp.where(in_head, q, jnp.zeros_like(q))
            s = jnp.concatenate(
                [lax.dot_general(qm, kr[:, cols], (((1,), (1,)), ((), ())),
                                 preferred_element_type=jnp.float32) for kr in k_refs], axis=1)
            if n_lat_blocks:
                s = jnp.concatenate([s[:, :lat_keys] + bias_ref[head], s[:, lat_keys:]], axis=1)
            m = jnp.max(s, axis=-1, keepdims=True)
            e = jnp.exp2(s - m)
            l = jnp.sum(e, axis=-1, keepdims=True)
            eb = e.astype(jnp.bfloat16)
            acc = None
            for j, vr in enumerate(v_refs):
                part = jnp.dot(eb[:, j * TOKEN_TILE:(j + 1) * TOKEN_TILE], vr[:, cols],
                               preferred_element_type=jnp.float32)
                acc = part if acc is None else acc + part
            outs.append(acc / l)
        o_ref[:, cols] = jnp.where(lane < HEAD_DIM, outs[0], outs[1]).astype(o_ref.dtype)


def _na_bias_table(rpb, rows):
    n_groups = rows // NA_ROWS
    heads = rpb.shape[0]
    padded = jnp.pad(rpb.astype(jnp.float32) * LOG2E, ((0, 0), (0, 0), (GRID_W, GRID_W)))
    toeplitz = jnp.stack([padded[:, :, GRID_W + WIN_W - 1 - qc:2 * GRID_W + WIN_W - 1 - qc]
                          for qc in range(GRID_W)], axis=2)
    qc = np.arange(GRID_W)[:, None]
    kc = np.arange(GRID_W)[None, :]
    col_start = np.clip(qc - WIN_W // 2, 0, GRID_W - WIN_W)
    col_valid = (kc >= col_start) & (kc < col_start + WIN_W)
    toeplitz = jnp.where(col_valid[None, None], toeplitz, NEG_BIG)
    masked = jnp.full((heads, GRID_W, GRID_W), NEG_BIG, jnp.float32)
    tables = []
    for g in (0, 1, n_groups - 1):
        ws = min(max(g - 1, 0), n_groups - 3) * NA_ROWS
        q_rows = []
        for qr in range(g * NA_ROWS, (g + 1) * NA_ROWS):
            row_start = min(max(qr - WIN_H // 2, 0), rows - WIN_H)
            blocks = [toeplitz[:, kr - qr + WIN_H - 1] if row_start <= kr < row_start + WIN_H else masked
                      for kr in range(ws, ws + NA_KEY_ROWS)]
            q_rows.append(jnp.concatenate(blocks, axis=-1))
        tables.append(jnp.concatenate(q_rows, axis=1))
    return jnp.stack(tables)


def _na_latent(q, k, v, bias, n_lat):
    b = q.shape[0]
    tm = TOKEN_TILE
    assert NA_ROWS * GRID_W == tm and NA_KEY_ROWS * GRID_W == 3 * tm
    ng = n_lat // tm
    ctx_blk = n_lat // tm

    def kv_spec(j):
        return pl.BlockSpec((None, tm, NA_WIDTH),
                            lambda bi, g: (bi, jnp.clip(g - 1, 0, ng - 3) + j, NA_COL_BLOCK))

    ctx_spec = pl.BlockSpec((None, tm, NA_WIDTH), lambda bi, g: (bi, ctx_blk, NA_COL_BLOCK))
    variant = lambda bi, g: ((g > 0).astype(jnp.int32) + (g == ng - 1).astype(jnp.int32), 0, 0, 0)
    return pl.pallas_call(
        functools.partial(_na_kernel, n_lat_blocks=3),
        grid=(b, ng),
        in_specs=[pl.BlockSpec((None, tm, NA_WIDTH), lambda bi, g: (bi, g, NA_COL_BLOCK)),
                  kv_spec(0), kv_spec(1), kv_spec(2), ctx_spec,
                  kv_spec(0), kv_spec(1), kv_spec(2), ctx_spec,
                  pl.BlockSpec((None, NA_HEADS, tm, 3 * tm), variant)],
        out_specs=pl.BlockSpec((None, tm, NA_WIDTH), lambda bi, g: (bi, g, 0)),
        out_shape=jax.ShapeDtypeStruct((b, n_lat, NA_WIDTH), jnp.bfloat16),
        compiler_params=_cparams(("parallel", "parallel")),
        name="na_latent",
    )(q, k, k, k, k, v, v, v, v, bias)


def _na_context(q, k, v, n_lat):
    b = q.shape[0]
    tm = TOKEN_TILE
    ctx_spec = pl.BlockSpec((None, tm, NA_WIDTH), lambda bi: (bi, n_lat // tm, NA_COL_BLOCK))
    return pl.pallas_call(
        functools.partial(_na_kernel, n_lat_blocks=0),
        grid=(b,),
        in_specs=[ctx_spec, ctx_spec, ctx_spec],
        out_specs=pl.BlockSpec((None, tm, NA_WIDTH), lambda bi: (bi, 0, 0)),
        out_shape=jax.ShapeDtypeStruct((b, tm, NA_WIDTH), jnp.bfloat16),
        compiler_params=_cparams(("parallel",)),
        name="na_context",
    )(q, k, v)


def _diff_kernel(q_ref, k_ref, v_ref, lam_ref, gain_ref, o_ref, m_ref, acc_ref, *, lambda_init):
    kk = pl.program_id(3)

    @pl.when(kk == 0)
    def _():
        m_ref[...] = jnp.full(m_ref.shape, NEG_BIG, jnp.float32)
        acc_ref[...] = jnp.zeros(acc_ref.shape, jnp.float32)

    lane = lax.broadcasted_iota(jnp.int32, (1, LANES), 1)
    reps = k_ref.shape[0] // LANES
    n_heads = q_ref.shape[1] // LANES
    for vh in range(n_heads):
        cols = slice(vh * LANES, (vh + 1) * LANES)
        q = q_ref[:, cols]
        k = k_ref[:, cols]
        v = v_ref[:, cols]
        v_ext = jnp.concatenate([v, jnp.ones_like(v)], axis=1)
        for hh in range(2):
            idx = 2 * vh + hh
            in_head = (lane >= hh * HEAD_DIM) & (lane < (hh + 1) * HEAD_DIM)
            qm = jnp.where(in_head, q, jnp.zeros_like(q))
            s = lax.dot_general(qm, k, (((1,), (1,)), ((), ())), preferred_element_type=jnp.float32)
            m_old = m_ref[idx]
            m_new = jnp.maximum(m_old, jnp.max(s, axis=-1, keepdims=True))
            alpha = jnp.exp2(m_old - m_new)
            p = jnp.exp2(s - jnp.concatenate([m_new] * reps, axis=1)).astype(jnp.bfloat16)
            pv = jnp.dot(p, v_ext, preferred_element_type=jnp.float32)
            acc_ref[idx] = jnp.concatenate([alpha, alpha], axis=1) * acc_ref[idx] + pv
            m_ref[idx] = m_new

    @pl.when(kk == pl.num_programs(3) - 1)
    def _():
        lp = lam_ref[...]
        lam = (jnp.exp(jnp.sum(lp[0:1] * lp[1:2], axis=-1, keepdims=True))
               - jnp.exp(jnp.sum(lp[2:3] * lp[3:4], axis=-1, keepdims=True)) + lambda_init)
        for vh in range(n_heads):
            a0 = acc_ref[2 * vh]
            a1 = acc_ref[2 * vh + 1]
            o = a0[:, :LANES] / a0[:, LANES:] - lam * (a1[:, :LANES] / a1[:, LANES:])
            o_ref[:, vh * LANES:(vh + 1) * LANES] = (
                _rms(o) * gain_ref[...] * (1.0 - lambda_init)).astype(o_ref.dtype)


def _diff_attention(q, k, v, lam_params, gain, lambda_init, *, q_rows, q_start, tq, k_rows, k_start, tk,
                    heads_per_step):
    b = q.shape[0]
    nq, nk = q_rows // tq, k_rows // tk
    qo, ko = q_start // tq, k_start // tk
    width = heads_per_step * LANES
    first = 0
    return pl.pallas_call(
        functools.partial(_diff_kernel, lambda_init=lambda_init),
        grid=(b, DIFF_HEADS // heads_per_step, nq, nk),
        in_specs=[pl.BlockSpec((None, tq, width), lambda bi, h, i, j: (bi, qo + i, first + h)),
                  pl.BlockSpec((None, tk, width), lambda bi, h, i, j: (bi, ko + j, first + h)),
                  pl.BlockSpec((None, tk, width), lambda bi, h, i, j: (bi, ko + j, first + h)),
                  pl.BlockSpec((4, HEAD_DIM), lambda bi, h, i, j: (0, 0)),
                  pl.BlockSpec((1, LANES), lambda bi, h, i, j: (0, 0))],
        out_specs=pl.BlockSpec((None, tq, width), lambda bi, h, i, j: (bi, i, h)),
        out_shape=jax.ShapeDtypeStruct((b, q_rows, DIFF_WIDTH), jnp.bfloat16),
        scratch_shapes=[pltpu.VMEM((2 * heads_per_step, tq, LANES), jnp.float32),
                        pltpu.VMEM((2 * heads_per_step, tq, 2 * LANES), jnp.float32)],
        compiler_params=_cparams(("parallel", "parallel", "parallel", "arbitrary")),
        name="diff_attention",
    )(q, k, v, lam_params, gain)


def _out_proj_kernel(x_ref, c_ref, na_ref, dif_ref, u_ref, up_ref, un_ref, bg_ref, cw_ref, w_ref, mod_ref,
                     *rest, n_lat_tiles, router):
    if router:
        rw_ref, x1_ref, h2_ref, route_ref = rest
    else:
        w1_ref, w3_ref, w2_ref, o_ref = rest
    i = pl.program_id(1)
    tm = x_ref.shape[0]
    first_of_seq = (i == 0) | (i == n_lat_tiles)
    last_of_seq = (i == n_lat_tiles - 1) | (i == n_lat_tiles)
    u = u_ref[...]
    up = jnp.where(first_of_seq, 0.0, up_ref[7:8, :])
    un = jnp.where(last_of_seq, 0.0, un_ref[0:1, :])
    row = lax.broadcasted_iota(jnp.int32, (tm, 1), 0)
    u_prev = jnp.where(row == 0, up, pltpu.roll(u, 1, axis=0))
    u_next = jnp.where(row == tm - 1, un, pltpu.roll(u, tm - 1, axis=0))
    conv = bg_ref[...] * (cw_ref[0:1, :] * u_prev + cw_ref[1:2, :] * u + cw_ref[2:3, :] * u_next)
    o1 = NA_WIDTH
    o2 = NA_WIDTH + DIFF_WIDTH
    mix = (jnp.dot(na_ref[...], w_ref[:o1, :], preferred_element_type=jnp.float32)
           + jnp.dot(dif_ref[...], w_ref[o1:o2, :], preferred_element_type=jnp.float32)
           + jnp.dot(conv.astype(jnp.bfloat16), w_ref[o2:, :], preferred_element_type=jnp.float32))
    x1 = jnp.where(i >= n_lat_tiles, c_ref[...], x_ref[...]) + mod_ref[2:3, :] * mix
    h2 = _rms(x1) * (1.0 + mod_ref[4:5, :]) + mod_ref[3:4, :]
    if not router:
        h = h2.astype(jnp.bfloat16)
        a = jnp.dot(h, w1_ref[...], preferred_element_type=jnp.float32)
        g = jnp.dot(h, w3_ref[...], preferred_element_type=jnp.float32)
        act = (a * jax.nn.sigmoid(a) * g).astype(jnp.bfloat16)
        y = jnp.dot(act, w2_ref[...], preferred_element_type=jnp.float32)
        o_ref[...] = x1 + mod_ref[5:6, :] * y
    else:
        x1_ref[...] = x1
        h2_ref[...] = _pack_halves(h2)
        h_hi = h2.astype(jnp.bfloat16)
        h_lo = (h2 - h_hi.astype(jnp.float32)).astype(jnp.bfloat16)
        logits = (jnp.dot(h_hi, rw_ref[0], preferred_element_type=jnp.float32)
                  + (jnp.dot(h_hi, rw_ref[1], preferred_element_type=jnp.float32)
                     + jnp.dot(h_lo, rw_ref[0], preferred_element_type=jnp.float32)))
        lane = lax.broadcasted_iota(jnp.int32, logits.shape, 1)
        logits = jnp.where(lane < N_EXPERTS, logits, NEG_BIG)
        m1 = jnp.max(logits, axis=-1, keepdims=True)
        i1 = jnp.min(jnp.where(logits == m1, lane, LANES), axis=-1, keepdims=True)
        rest_l = jnp.where(lane == i1, NEG_BIG, logits)
        m2 = jnp.max(rest_l, axis=-1, keepdims=True)
        i2 = jnp.min(jnp.where(rest_l == m2, lane, LANES), axis=-1, keepdims=True)
        e2 = jnp.exp(m2 - m1)
        w1 = 1.0 / (1.0 + e2)
        route_ref[...] = jnp.where(
            lane == 0, i1.astype(jnp.float32),
            jnp.where(lane == 1, i2.astype(jnp.float32),
                      jnp.where(lane == 2, w1, jnp.where(lane == 3, e2 * w1, 0.0))))


def _out_proj(lat, ctx, ctx_block, na, dif, u, bg, conv_w, w_bf16, modtab, n_lat, n_rows,
              router_w=None, ffn=None):
    b, _, d = lat.shape
    tm = TOKEN_TILE
    nlt = n_lat // tm
    sub = 8
    n_sub = u.shape[1] // sub
    tok = lambda width: pl.BlockSpec((None, tm, width), lambda bi, i: (bi, i, 0))
    in_specs = [*_stream_specs(lat, n_lat, ctx_block), tok(NA_WIDTH), tok(DIFF_WIDTH), tok(CONV_WIDTH),
                pl.BlockSpec((None, sub, CONV_WIDTH),
                             lambda bi, i: (bi, jnp.maximum(i * (tm // sub) - 1, 0), 0)),
                pl.BlockSpec((None, sub, CONV_WIDTH),
                             lambda bi, i: (bi, jnp.minimum((i + 1) * (tm // sub), n_sub - 1), 0)),
                tok(CONV_WIDTH),
                pl.BlockSpec((3, CONV_WIDTH), lambda bi, i: (0, 0)),
                pl.BlockSpec((d, d), lambda bi, i: (0, 0)),
                pl.BlockSpec((None, 6, d), lambda bi, i: (bi * 2 + (i >= nlt).astype(jnp.int32), 0, 0))]
    args = [lat, ctx, na, dif, u, u, u, bg, conv_w, w_bf16, modtab]
    assert (router_w is None) != (ffn is None)
    if router_w is None:
        const = lambda w: pl.BlockSpec(w.shape, lambda bi, i: (0, 0), pipeline_mode=pl.Buffered(1))
        in_specs += [const(w) for w in ffn]
        out_specs = tok(d)
        out_shape = jax.ShapeDtypeStruct((b, n_rows, d), jnp.float32)
        args += list(ffn)
    else:
        in_specs.append(pl.BlockSpec((2, d, LANES), lambda bi, i: (0, 0, 0)))
        steps = n_rows // tm
        out_specs = [tok(d), pl.BlockSpec((tm, d // 2), lambda bi, i: (bi * steps + i, 0)), tok(LANES)]
        out_shape = [jax.ShapeDtypeStruct((b, n_rows, d), jnp.float32),
                     jax.ShapeDtypeStruct((b * n_rows, d // 2), jnp.uint32),
                     jax.ShapeDtypeStruct((b, n_rows, LANES), jnp.float32)]
        args.append(router_w)
    return pl.pallas_call(
        functools.partial(_out_proj_kernel, n_lat_tiles=nlt, router=router_w is not None),
        grid=(b, n_rows // tm),
        in_specs=in_specs, out_specs=out_specs, out_shape=out_shape,
        compiler_params=_cparams(("parallel", "parallel")),
        name="out_proj_ffn" if router_w is None else "out_proj_route",
    )(*args)


MOE_TILE = 512
MOE_FF_CHUNK = 896
TOP_K = 2


def _route_plan(eid, n_exp, n_tiles):
    experts = jnp.arange(n_exp, dtype=jnp.int32)
    counts = jnp.sum((eid[:, None] == experts[None, :]).astype(jnp.int32), axis=0)
    tiles = (counts + MOE_TILE - 1) // MOE_TILE
    tile_end = jnp.cumsum(tiles)
    n_used = tile_end[-1]
    j = jnp.arange(n_tiles, dtype=jnp.int32)
    owner = jnp.sum((j[:, None] >= tile_end[None, :]).astype(jnp.int32), axis=1)
    last_owner = jnp.sum((n_used - 1 >= tile_end).astype(jnp.int32))
    tile_expert = jnp.where(j < n_used, owner, last_owner).astype(jnp.int32)
    meta = jnp.concatenate([(tile_end - tiles) * MOE_TILE, counts, tile_end * MOE_TILE,
                            n_used[None]]).astype(jnp.int32)
    return meta, tile_expert


def _slot_kernel(eid_ref, start_ref, tri_ref, sot_ref, wrow_ref, win_ref, next_ref):
    @pl.when(pl.program_id(0) == 0)
    def _():
        next_ref[...] = start_ref[...]

    e_row = eid_ref[...]
    expert = lax.broadcasted_iota(jnp.int32, (next_ref.shape[0], e_row.shape[1]), 0)
    hit = e_row == expert
    onehot = jnp.where(hit, 1.0, 0.0)
    before = jnp.dot(onehot.astype(jnp.bfloat16), tri_ref[...], preferred_element_type=jnp.float32)
    nxt = next_ref[:, 0:1]
    blk = jnp.floor(nxt * (1.0 / TOKEN_TILE))
    slot = before + nxt
    sot_ref[...] = jnp.sum(jnp.where(hit, slot, 0.0), axis=0, keepdims=True).astype(jnp.int32)
    wrow = expert.astype(jnp.float32) * (2 * TOKEN_TILE) + slot - blk * TOKEN_TILE
    wrow_ref[...] = jnp.sum(jnp.where(hit, wrow, 0.0), axis=0, keepdims=True).astype(jnp.int32)
    win_ref[...] = jnp.broadcast_to(blk, win_ref.shape).astype(jnp.int32)
    next_ref[...] = next_ref[...] + jnp.sum(onehot, axis=1, keepdims=True)


def _slots(eid, start, n_exp):
    n_assign = TOKEN_TILE * TOP_K
    steps = eid.shape[0] // n_assign
    assert n_exp == 8, "experts are laid along the eight sublanes"
    tri = jnp.asarray(np.triu(np.ones((n_assign, n_assign), np.float32), k=1), jnp.bfloat16)
    row = pl.BlockSpec((None, 1, n_assign), lambda i: (i, 0, 0))
    return pl.pallas_call(
        _slot_kernel,
        grid=(steps,),
        in_specs=[row, pl.BlockSpec((n_exp, LANES), lambda i: (0, 0)),
                  pl.BlockSpec((n_assign, n_assign), lambda i: (0, 0))],
        out_specs=[row, row, pl.BlockSpec((None, n_exp, LANES), lambda i: (i, 0, 0))],
        out_shape=[jax.ShapeDtypeStruct((steps, 1, n_assign), jnp.int32),
                   jax.ShapeDtypeStruct((steps, 1, n_assign), jnp.int32),
                   jax.ShapeDtypeStruct((steps, n_exp, LANES), jnp.int32)],
        scratch_shapes=[pltpu.VMEM((n_exp, LANES), jnp.float32)],
        compiler_params=_cparams(("arbitrary",)),
        name="moe_slots",
    )(eid.reshape(steps, 1, n_assign),
      jnp.broadcast_to(start.astype(jnp.float32)[:, None], (n_exp, LANES)), tri)


INVERT_CHUNK = 4096


def _invert_kernel(meta_ref, sot_ref, tos_ref, *, n_exp, n_slots):
    i = pl.program_id(0)
    n_assign = sot_ref.shape[1]

    def clear(s, carry):
        tos_ref[s] = 0
        return carry

    @pl.when(i == 0)
    def _():
        for e in range(n_exp):
            lax.fori_loop(meta_ref[e] + meta_ref[n_exp + e], meta_ref[2 * n_exp + e], clear, 0)
        lax.fori_loop(meta_ref[3 * n_exp] * MOE_TILE, n_slots, clear, 0)

    first = i * (n_assign // TOP_K)

    def put(t, carry):
        for k in range(TOP_K):
            tos_ref[sot_ref[0, TOP_K * t + k]] = first + t
        return carry

    lax.fori_loop(0, n_assign // TOP_K, put, 0, unroll=8)


def _invert(meta, sot, n_exp, n_slots):
    steps = sot.size // INVERT_CHUNK
    blocked = pl.BlockSpec((None, 1, INVERT_CHUNK), lambda i, meta: (i, 0, 0), memory_space=pltpu.SMEM)
    return pl.pallas_call(
        functools.partial(_invert_kernel, n_exp=n_exp, n_slots=n_slots),
        grid_spec=pltpu.PrefetchScalarGridSpec(
            num_scalar_prefetch=1, grid=(steps,),
            in_specs=[blocked], out_specs=pl.BlockSpec(memory_space=pltpu.SMEM)),
        out_shape=jax.ShapeDtypeStruct((n_slots,), jnp.int32),
        compiler_params=_cparams(("arbitrary",)),
        name="moe_invert",
    )(meta, sot.reshape(steps, 1, INVERT_CHUNK))


def _experts_kernel(texp_ref, nused_ref, tos_ref, tab_ref, w1_ref, w3_ref, w2_ref, ys_ref,
                    xp_ref, xs_ref, acc_ref):
    j = pl.program_id(0)
    c = pl.program_id(1)
    used = j < nused_ref[0]
    last = c == pl.num_programs(1) - 1

    @pl.when(used & (c == 0))
    def _():
        def gather(s, carry):
            xp_ref[pl.ds(s, 1), :] = tab_ref[pl.ds(tos_ref[0, s], 1), :]
            return carry

        lax.fori_loop(0, MOE_TILE, gather, 0, unroll=8)
        xs_ref[...] = _unpack_halves(xp_ref[...]).astype(xs_ref.dtype)
        acc_ref[...] = jnp.zeros(acc_ref.shape, jnp.float32)

    @pl.when(used)
    def _():
        h = xs_ref[...]
        a = jnp.dot(h, w1_ref[...], preferred_element_type=jnp.float32)
        g = jnp.dot(h, w3_ref[...], preferred_element_type=jnp.float32)
        act = (a * jax.nn.sigmoid(a) * g).astype(jnp.bfloat16)
        acc_ref[...] += jnp.dot(act, w2_ref[...], preferred_element_type=jnp.float32)

    @pl.when(used & last)
    def _():
        ys_ref[...] = _pack_halves(acc_ref[...])

    @pl.when(jnp.logical_not(used) & last)
    def _():
        ys_ref[...] = _pack_halves(jnp.zeros(acc_ref.shape, jnp.float32))


def _experts(tile_expert, n_used, tos, table, w1, w3, w2):
    n_tiles = tos.shape[0]
    n_tok, half = table.shape
    d = 2 * half
    ff = w1.shape[2]
    fc = MOE_FF_CHUNK
    nc = ff // fc
    assert ff % fc == 0
    chunk = lambda j, c, te, nu: jnp.where(j < nu[0], c, nc - 1)
    return pl.pallas_call(
        _experts_kernel,
        grid_spec=pltpu.PrefetchScalarGridSpec(
            num_scalar_prefetch=2, grid=(n_tiles, nc),
            in_specs=[pl.BlockSpec((None, 1, MOE_TILE), lambda j, c, te, nu: (j, 0, 0),
                                   memory_space=pltpu.SMEM),
                      pl.BlockSpec((n_tok, half), lambda j, c, te, nu: (0, 0),
                                   pipeline_mode=pl.Buffered(1)),
                      pl.BlockSpec((None, d, fc), lambda j, c, te, nu: (te[j], 0, chunk(j, c, te, nu))),
                      pl.BlockSpec((None, d, fc), lambda j, c, te, nu: (te[j], 0, chunk(j, c, te, nu))),
                      pl.BlockSpec((None, fc, d), lambda j, c, te, nu: (te[j], chunk(j, c, te, nu), 0))],
            out_specs=pl.BlockSpec((MOE_TILE, half), lambda j, c, te, nu: (j, 0)),
            scratch_shapes=[pltpu.VMEM((MOE_TILE, half), jnp.uint32),
                            pltpu.VMEM((MOE_TILE, d), jnp.bfloat16),
                            pltpu.VMEM((MOE_TILE, d), jnp.float32)]),
        out_shape=jax.ShapeDtypeStruct((n_tiles * MOE_TILE, half), jnp.uint32),
        compiler_params=_cparams(("arbitrary", "arbitrary")),
        name="moe_experts",
    )(tile_expert, n_used, tos, table, w1, w3, w2)


def _combine_kernel(win_ref, x1_ref, route_ref, wrow_ref, *rest, n_exp):
    ys_refs = rest[:2 * n_exp]
    mod_ref, gain_ref, o_ref, w_ref, r_ref = rest[2 * n_exp:]
    tm = x1_ref.shape[0]
    for blk, ys_ref in enumerate(ys_refs):
        w_ref[pl.ds(blk * tm, tm), :] = ys_ref[...]

    def fetch(r, carry):
        for k in range(TOP_K):
            r_ref[k, pl.ds(r, 1), :] = w_ref[pl.ds(wrow_ref[0, TOP_K * r + k], 1), :]
        return carry

    lax.fori_loop(0, tm, fetch, 0, unroll=8)
    route = route_ref[...]
    y = route[:, 2:3] * _unpack_halves(r_ref[0]) + route[:, 3:4] * _unpack_halves(r_ref[1])
    x2 = x1_ref[...] + mod_ref[5:6, :] * y
    o_ref[...] = _rms(x2) * gain_ref[...]


def _combine_final(win, x1, route, wrow3, ys, modtab, final_gain, tiles_per_batch, n_exp):
    n_tok, d = x1.shape
    tm = TOKEN_TILE
    n_assign = tm * TOP_K
    ys_specs = [pl.BlockSpec((tm, d // 2), lambda i, win, e=e, jj=jj: (win[i * n_exp + e] + jj, 0))
                for e in range(n_exp) for jj in range(2)]
    smem_blk = pl.BlockSpec((None, 1, n_assign), lambda i, win: (i, 0, 0), memory_space=pltpu.SMEM)
    return pl.pallas_call(
        functools.partial(_combine_kernel, n_exp=n_exp),
        grid_spec=pltpu.PrefetchScalarGridSpec(
            num_scalar_prefetch=1, grid=(n_tok // tm,),
            in_specs=[pl.BlockSpec((tm, d), lambda i, win: (i, 0)),
                      pl.BlockSpec((tm, LANES), lambda i, win: (i, 0)),
                      smem_blk, *ys_specs,
                      pl.BlockSpec((None, 6, d), lambda i, win: ((i // tiles_per_batch) * 2, 0, 0)),
                      pl.BlockSpec((1, d), lambda i, win: (0, 0))],
            out_specs=pl.BlockSpec((tm, d), lambda i, win: (i, 0)),
            scratch_shapes=[pltpu.VMEM((2 * n_exp * tm, d // 2), jnp.uint32),
                            pltpu.VMEM((TOP_K, tm, d // 2), jnp.uint32)]),
        out_shape=jax.ShapeDtypeStruct((n_tok, d), jnp.float32),
        compiler_params=_cparams(("arbitrary",)),
        name="moe_combine",
    )(win, x1, route, wrow3, *([ys] * (2 * n_exp)), modtab, final_gain.reshape(1, d))


def _moe_ffn_final(x1, h2p, route, w1, w3, w2, modtab, final_gain):
    b, n, d = x1.shape
    n_tok = b * n
    n_exp = w1.shape[0]
    n_tiles = n_tok * TOP_K // MOE_TILE + n_exp + 1
    eid = route[..., :TOP_K].astype(jnp.int32).reshape(n_tok * TOP_K)
    meta, tile_expert = _route_plan(eid, n_exp, n_tiles)
    sot3, wrow3, win = _slots(eid, meta[:n_exp], n_exp)
    tos = _invert(meta, sot3, n_exp, n_tiles * MOE_TILE)
    ys = _experts(tile_expert, meta[3 * n_exp:], tos.reshape(n_tiles, 1, MOE_TILE),
                  h2p, w1, w3, w2)
    out = _combine_final(win[:, :, 0].reshape(-1), x1.reshape(n_tok, d), route.reshape(n_tok, LANES),
                         wrow3, ys, modtab, final_gain, n // TOKEN_TILE, n_exp)
    return out.reshape(b, n, d)


def _rope_tables(n_lat, n_ctx):
    t = jnp.arange(n_lat, dtype=jnp.int32)
    n_freq = HEAD_DIM // 4
    inv_freq = ROPE_BASE ** (-jnp.arange(n_freq, dtype=jnp.float32) / n_freq)
    ang = jnp.concatenate([(t // GRID_W).astype(jnp.float32)[:, None] * inv_freq,
                           (t % GRID_W).astype(jnp.float32)[:, None] * inv_freq], axis=-1)
    cos, sin = jnp.cos(ang), jnp.sin(ang)
    reps = LANES // HEAD_DIM
    cos_t = jnp.tile(jnp.concatenate([cos, cos], axis=-1), (1, reps))
    sin_t = jnp.tile(jnp.concatenate([-sin, sin], axis=-1), (1, reps))
    cos_t = jnp.concatenate([cos_t, jnp.ones((n_ctx, LANES), jnp.float32)], axis=0)
    sin_t = jnp.concatenate([sin_t, jnp.zeros((n_ctx, LANES), jnp.float32)], axis=0)
    return cos_t, sin_t


def kernel(x, c, ctx, c_ctx, ada_w, ada_b, w_in, w_out, na_rpb, diff_lambda, diff_subln, conv_w,
           ffn_w1, ffn_w3, ffn_w2, router_w, moe_w1, moe_w3, moe_w2, final_gain):
    b, n, d = x.shape
    n_ctx = ctx.shape[1]
    depth = w_in.shape[0]
    assert d == D_MODEL and n % TOKEN_TILE == 0 and n_ctx == TOKEN_TILE and b + 1 <= 8
    assert depth == 2, "layer 0 dense with a context stream, layer 1 routed and final"
    bf = jnp.bfloat16

    stream = (x, ctx, 0)
    cvec = jnp.zeros((8, d), jnp.float32).at[:b].set(c).at[b].set(c_ctx)
    mod = _modulation(cvec, ada_w, ada_b).reshape(depth, 8, 6, d)
    cos_t, sin_t = _rope_tables(n, n_ctx)
    rows = n // GRID_W

    out = None
    for i in range(depth):
        lambda_init = 0.8 - 0.6 * math.exp(-0.3 * i)
        ctx_out = i < depth - 1
        modtab = jnp.stack([mod[i, :b], jnp.broadcast_to(mod[i, b], (b, 6, d))], axis=1).reshape(2 * b, 6, d)
        q, k, v, u, bg = _in_proj(*stream, modtab, cos_t, sin_t, w_in[i].astype(bf), n)
        bias = _na_bias_table(na_rpb[i], rows)
        na = _na_latent(q, k, v, bias, n)
        gain = diff_subln[i].reshape(1, 2 * HEAD_DIM)
        dif = _diff_attention(q, k, v, diff_lambda[i], gain, lambda_init,
                              q_rows=n, q_start=0, tq=1024, k_rows=n + n_ctx, k_start=0, tk=1408,
                              heads_per_step=4)
        if ctx_out:
            na_c = _na_context(q, k, v, n)
            dif_c = _diff_attention(q, k, v, diff_lambda[i], gain, lambda_init,
                                    q_rows=n_ctx, q_start=n, tq=n_ctx, k_rows=n_ctx, k_start=n, tk=n_ctx,
                                    heads_per_step=4)
            na = jnp.concatenate([na, na_c], axis=1)
            dif = jnp.concatenate([dif, dif_c], axis=1)
            m = i // 2
            x_all = _out_proj(*stream, na, dif, u, bg, conv_w[i], w_out[i].astype(bf), modtab, n, n + n_ctx,
                              ffn=(ffn_w1[m].astype(bf), ffn_w3[m].astype(bf), ffn_w2[m].astype(bf)))
            stream = (x_all, x_all, n // TOKEN_TILE)
        else:
            m = i // 2
            rw = jnp.zeros((d, LANES), jnp.float32).at[:, :N_EXPERTS].set(router_w[m])
            rw_hi = rw.astype(bf)
            rw = jnp.stack([rw_hi, (rw - rw_hi.astype(jnp.float32)).astype(bf)])
            x1, h2p, route = _out_proj(*stream, na, dif, u, bg, conv_w[i], w_out[i].astype(bf), modtab, n, n,
                                       router_w=rw)
            out = _moe_ffn_final(x1, h2p, route, moe_w1[m].astype(bf), moe_w3[m].astype(bf),
                                 moe_w2[m].astype(bf), modtab, final_gain)
    return out
```

```python
import functools
import math

import numpy as np
import jax
import jax.numpy as jnp
from jax import lax
from jax.experimental import pallas as pl
from jax.experimental.pallas import tpu as pltpu

D_MODEL = 1024
GRID_W = 64
HEAD_DIM = 64
NA_HEADS = 4
NA_WIDTH = NA_HEADS * HEAD_DIM
WIN_H = 8
WIN_W = 16
DIFF_HEADS = 4
DIFF_WIDTH = DIFF_HEADS * 2 * HEAD_DIM
CONV_WIDTH = 256
QKV_COLS = NA_WIDTH + DIFF_WIDTH
NA_COL_BLOCK = DIFF_WIDTH // NA_WIDTH
IN_COLS = 3 * QKV_COLS + 3 * CONV_WIDTH
ROPE_BASE = 10000.0
N_EXPERTS = 8
EPS = 1e-6
NEG_BIG = -1e30
LOG2E = math.log2(math.e)

LANES = 128
TOKEN_TILE = 256
NA_ROWS = 4
NA_KEY_ROWS = NA_ROWS + WIN_H
VMEM_LIMIT = 56 * 1024 * 1024


def _cparams(sem):
    return pltpu.CompilerParams(dimension_semantics=sem, vmem_limit_bytes=VMEM_LIMIT)


def _rms(x):
    return x * lax.rsqrt(jnp.mean(x * x, axis=-1, keepdims=True) + EPS)


def _pack_halves(x):
    half = x.shape[1] // 2
    return pltpu.pack_elementwise([x[:, :half], x[:, half:]], packed_dtype=jnp.bfloat16)


def _unpack_halves(p):
    parts = [pltpu.unpack_elementwise(p, index=i, packed_dtype=jnp.bfloat16, unpacked_dtype=jnp.float32)
             for i in range(2)]
    return jnp.concatenate(parts, axis=1)


def _mod_kernel(c_ref, w_ref, b_ref, o_ref):
    cv = c_ref[...]
    s = cv * jax.nn.sigmoid(cv)
    o_ref[...] = jnp.dot(s, w_ref[...], preferred_element_type=jnp.float32,
                         precision=lax.Precision.HIGHEST) + b_ref[...]


def _modulation(cvec, ada_w, ada_b):
    depth, d, cols = ada_w.shape
    cb = 1536
    return pl.pallas_call(
        _mod_kernel,
        grid=(depth, cols // cb),
        in_specs=[pl.BlockSpec((8, d), lambda i, j: (0, 0)),
                  pl.BlockSpec((None, d, cb), lambda i, j: (i, 0, j)),
                  pl.BlockSpec((None, 1, cb), lambda i, j: (i, 0, j))],
        out_specs=pl.BlockSpec((None, 8, cb), lambda i, j: (i, 0, j)),
        out_shape=jax.ShapeDtypeStruct((depth, 8, cols), jnp.float32),
        compiler_params=_cparams(("parallel", "parallel")),
        name="modulation",
    )(cvec, ada_w, ada_b.reshape(depth, 1, cols))


def _rope(z, cos, sin_signed):
    width = z.shape[1]
    reps = width // LANES
    c = jnp.concatenate([cos] * reps, axis=1)
    s = jnp.concatenate([sin_signed] * reps, axis=1)
    lane = lax.broadcasted_iota(jnp.int32, (1, width), 1)
    first_half = (lane % HEAD_DIM) < (HEAD_DIM // 2)
    swapped = jnp.where(first_half,
                        pltpu.roll(z, width - HEAD_DIM // 2, axis=1),
                        pltpu.roll(z, HEAD_DIM // 2, axis=1))
    return z * c + swapped * s


def _stream_specs(lat, n_lat, ctx_block):
    tm = TOKEN_TILE
    nlt = n_lat // tm
    d = lat.shape[-1]
    return [pl.BlockSpec((None, tm, d), lambda bi, i: (bi, jnp.minimum(i, nlt - 1), 0)),
            pl.BlockSpec((None, tm, d), lambda bi, i: (bi, ctx_block, 0))]


def _in_proj_kernel(x_ref, c_ref, mod_ref, cos_ref, sin_ref, w_ref, q_ref, k_ref, v_ref, u_ref, bg_ref,
                    *, n_lat_tiles):
    x = jnp.where(pl.program_id(1) >= n_lat_tiles, c_ref[...], x_ref[...])
    h = _rms(x) * (1.0 + mod_ref[1:2, :]) + mod_ref[0:1, :]
    p = jnp.dot(h.astype(jnp.bfloat16), w_ref[...], preferred_element_type=jnp.float32)
    cos = cos_ref[...]
    sin = sin_ref[...]
    scale = HEAD_DIM ** -0.5 * LOG2E
    q_ref[:, :DIFF_WIDTH] = (_rope(p[:, NA_WIDTH:QKV_COLS], cos, sin) * scale).astype(q_ref.dtype)
    q_ref[:, DIFF_WIDTH:] = (p[:, :NA_WIDTH] * scale).astype(q_ref.dtype)
    k_ref[:, :DIFF_WIDTH] = _rope(p[:, QKV_COLS + NA_WIDTH:2 * QKV_COLS], cos, sin).astype(k_ref.dtype)
    k_ref[:, DIFF_WIDTH:] = p[:, QKV_COLS:QKV_COLS + NA_WIDTH].astype(k_ref.dtype)
    o = 2 * QKV_COLS
    v_ref[:, :DIFF_WIDTH] = p[:, o + NA_WIDTH:o + QKV_COLS].astype(v_ref.dtype)
    v_ref[:, DIFF_WIDTH:] = p[:, o:o + NA_WIDTH].astype(v_ref.dtype)
    o = 3 * QKV_COLS
    xin = p[:, o:o + CONV_WIDTH]
    bg_ref[...] = p[:, o + CONV_WIDTH:o + 2 * CONV_WIDTH]
    u_ref[...] = p[:, o + 2 * CONV_WIDTH:o + 3 * CONV_WIDTH] * xin


def _in_proj(lat, ctx, ctx_block, modtab, cos_t, sin_t, w_bf16, n_lat):
    b, _, d = lat.shape
    tm = TOKEN_TILE
    nlt = n_lat // tm
    t = n_lat + tm
    tok = lambda width: pl.BlockSpec((None, tm, width), lambda bi, i: (bi, i, 0))
    return pl.pallas_call(
        functools.partial(_in_proj_kernel, n_lat_tiles=nlt),
        grid=(b, t // tm),
        in_specs=[*_stream_specs(lat, n_lat, ctx_block),
                  pl.BlockSpec((None, 6, d), lambda bi, i: (bi * 2 + (i >= nlt).astype(jnp.int32), 0, 0)),
                  pl.BlockSpec((tm, LANES), lambda bi, i: (i, 0)),
                  pl.BlockSpec((tm, LANES), lambda bi, i: (i, 0)),
                  pl.BlockSpec((d, IN_COLS), lambda bi, i: (0, 0))],
        out_specs=[tok(QKV_COLS), tok(QKV_COLS), tok(QKV_COLS), tok(CONV_WIDTH), tok(CONV_WIDTH)],
        out_shape=[jax.ShapeDtypeStruct((b, t, QKV_COLS), jnp.bfloat16)] * 3
        + [jax.ShapeDtypeStruct((b, t, CONV_WIDTH), jnp.float32)] * 2,
        compiler_params=_cparams(("parallel", "parallel")),
        name="in_proj",
    )(lat, ctx, modtab, cos_t, sin_t, w_bf16)


def _na_kernel(*refs, n_lat_blocks):
    q_ref = refs[0]
    k_refs = refs[1:2 + n_lat_blocks]
    v_refs = refs[2 + n_lat_blocks:3 + 2 * n_lat_blocks]
    if n_lat_blocks:
        bias_ref, o_ref = refs[3 + 2 * n_lat_blocks:]
    else:
        (o_ref,) = refs[3 + 2 * n_lat_blocks:]
    lane = lax.broadcasted_iota(jnp.int32, (1, LANES), 1)
    lat_keys = n_lat_blocks * TOKEN_TILE
    for pair in range(NA_HEADS // 2):
        cols = slice(pair * LANES, (pair + 1) * LANES)
        q = q_ref[:, cols]
        outs = []
        for hh in range(2):
            head = 2 * pair + hh
            in_head = (lane >= hh * HEAD_DIM) & (lane < (hh + 1) * HEAD_DIM)
            qm = jnp.where(in_head, q, jnp.zeros_like(q))
            s = jnp.concatenate(
                [lax.dot_general(qm, kr[:, cols], (((1,), (1,)), ((), ())),
                                 preferred_element_type=jnp.float32) for kr in k_refs], axis=1)
            if n_lat_blocks:
                s = jnp.concatenate([s[:, :lat_keys] + bias_ref[head], s[:, lat_keys:]], axis=1)
            m = jnp.max(s, axis=-1, keepdims=True)
            eb = jnp.exp2(s - m).astype(jnp.bfloat16)
            acc = None
            for j, vr in enumerate(v_refs):
                v = vr[:, cols]
                part = jnp.dot(eb[:, j * TOKEN_TILE:(j + 1) * TOKEN_TILE],
                               jnp.concatenate([v, jnp.ones_like(v)], axis=1),
                               preferred_element_type=jnp.float32)
                acc = part if acc is None else acc + part
            outs.append(acc[:, :LANES] / acc[:, LANES:])
        o_ref[:, cols] = jnp.where(lane < HEAD_DIM, outs[0], outs[1]).astype(o_ref.dtype)


def _na_bias_table(rpb, rows):
    n_groups = rows // NA_ROWS
    heads = rpb.shape[0]
    padded = jnp.pad(rpb.astype(jnp.float32) * LOG2E, ((0, 0), (0, 0), (GRID_W, GRID_W)))
    toeplitz = jnp.stack([padded[:, :, GRID_W + WIN_W - 1 - qc:2 * GRID_W + WIN_W - 1 - qc]
                          for qc in range(GRID_W)], axis=2)
    qc = np.arange(GRID_W)[:, None]
    kc = np.arange(GRID_W)[None, :]
    col_start = np.clip(qc - WIN_W // 2, 0, GRID_W - WIN_W)
    col_valid = (kc >= col_start) & (kc < col_start + WIN_W)
    toeplitz = jnp.where(col_valid[None, None], toeplitz, NEG_BIG)
    masked = jnp.full((heads, GRID_W, GRID_W), NEG_BIG, jnp.float32)
    tables = []
    for g in (0, 1, n_groups - 1):
        ws = min(max(g - 1, 0), n_groups - 3) * NA_ROWS
        q_rows = []
        for qr in range(g * NA_ROWS, (g + 1) * NA_ROWS):
            row_start = min(max(qr - WIN_H // 2, 0), rows - WIN_H)
            blocks = [toeplitz[:, kr - qr + WIN_H - 1] if row_start <= kr < row_start + WIN_H else masked
                      for kr in range(ws, ws + NA_KEY_ROWS)]
            q_rows.append(jnp.concatenate(blocks, axis=-1))
        tables.append(jnp.concatenate(q_rows, axis=1))
    return jnp.stack(tables)


def _na_latent(q, k, v, bias, n_lat):
    b = q.shape[0]
    tm = TOKEN_TILE
    assert NA_ROWS * GRID_W == tm and NA_KEY_ROWS * GRID_W == 3 * tm
    ng = n_lat // tm
    ctx_blk = n_lat // tm

    def kv_spec(j):
        return pl.BlockSpec((None, tm, NA_WIDTH),
                            lambda bi, g: (bi, jnp.clip(g - 1, 0, ng - 3) + j, NA_COL_BLOCK))

    ctx_spec = pl.BlockSpec((None, tm, NA_WIDTH), lambda bi, g: (bi, ctx_blk, NA_COL_BLOCK))
    variant = lambda bi, g: ((g > 0).astype(jnp.int32) + (g == ng - 1).astype(jnp.int32), 0, 0, 0)
    return pl.pallas_call(
        functools.partial(_na_kernel, n_lat_blocks=3),
        grid=(b, ng),
        in_specs=[pl.BlockSpec((None, tm, NA_WIDTH), lambda bi, g: (bi, g, NA_COL_BLOCK)),
                  kv_spec(0), kv_spec(1), kv_spec(2), ctx_spec,
                  kv_spec(0), kv_spec(1), kv_spec(2), ctx_spec,
                  pl.BlockSpec((None, NA_HEADS, tm, 3 * tm), variant)],
        out_specs=pl.BlockSpec((None, tm, NA_WIDTH), lambda bi, g: (bi, g, 0)),
        out_shape=jax.ShapeDtypeStruct((b, n_lat, NA_WIDTH), jnp.bfloat16),
        compiler_params=_cparams(("parallel", "parallel")),
        name="na_latent",
    )(q, k, k, k, k, v, v, v, v, bias)


def _na_context(q, k, v, n_lat):
    b = q.shape[0]
    tm = TOKEN_TILE
    ctx_spec = pl.BlockSpec((None, tm, NA_WIDTH), lambda bi: (bi, n_lat // tm, NA_COL_BLOCK))
    return pl.pallas_call(
        functools.partial(_na_kernel, n_lat_blocks=0),
        grid=(b,),
        in_specs=[ctx_spec, ctx_spec, ctx_spec],
        out_specs=pl.BlockSpec((None, tm, NA_WIDTH), lambda bi: (bi, 0, 0)),
        out_shape=jax.ShapeDtypeStruct((b, tm, NA_WIDTH), jnp.bfloat16),
        compiler_params=_cparams(("parallel",)),
        name="na_context",
    )(q, k, v)


def _diff_kernel(q_ref, k_ref, v_ref, lam_ref, gain_ref, o_ref, m_ref, acc_ref, *, lambda_init):
    kk = pl.program_id(3)

    @pl.when(kk == 0)
    def _():
        m_ref[...] = jnp.full(m_ref.shape, NEG_BIG, jnp.float32)
        acc_ref[...] = jnp.zeros(acc_ref.shape, jnp.float32)

    lane = lax.broadcasted_iota(jnp.int32, (1, LANES), 1)
    reps = k_ref.shape[0] // LANES
    n_heads = q_ref.shape[1] // LANES
    for vh in range(n_heads):
        cols = slice(vh * LANES, (vh + 1) * LANES)
        q = q_ref[:, cols]
        k = k_ref[:, cols]
        v = v_ref[:, cols]
        v_ext = jnp.concatenate([v, jnp.ones_like(v)], axis=1)
        for hh in range(2):
            idx = 2 * vh + hh
            in_head = (lane >= hh * HEAD_DIM) & (lane < (hh + 1) * HEAD_DIM)
            qm = jnp.where(in_head, q, jnp.zeros_like(q))
            s = lax.dot_general(qm, k, (((1,), (1,)), ((), ())), preferred_element_type=jnp.float32)
            m_old = m_ref[idx]
            m_new = jnp.maximum(m_old, jnp.max(s, axis=-1, keepdims=True))
            alpha = jnp.exp2(m_old - m_new)
            p = jnp.exp2(s - jnp.concatenate([m_new] * reps, axis=1)).astype(jnp.bfloat16)
            pv = jnp.dot(p, v_ext, preferred_element_type=jnp.float32)
            acc_ref[idx] = jnp.concatenate([alpha, alpha], axis=1) * acc_ref[idx] + pv
            m_ref[idx] = m_new

    @pl.when(kk == pl.num_programs(3) - 1)
    def _():
        lp = lam_ref[...]
        lam = (jnp.exp(jnp.sum(lp[0:1] * lp[1:2], axis=-1, keepdims=True))
               - jnp.exp(jnp.sum(lp[2:3] * lp[3:4], axis=-1, keepdims=True)) + lambda_init)
        for vh in range(n_heads):
            a0 = acc_ref[2 * vh]
            a1 = acc_ref[2 * vh + 1]
            o = a0[:, :LANES] / a0[:, LANES:] - lam * (a1[:, :LANES] / a1[:, LANES:])
            o_ref[:, vh * LANES:(vh + 1) * LANES] = (
                _rms(o) * gain_ref[...] * (1.0 - lambda_init)).astype(o_ref.dtype)


def _diff_attention(q, k, v, lam_params, gain, lambda_init, *, q_rows, q_start, tq, k_rows, k_start, tk,
                    heads_per_step):
    b = q.shape[0]
    nq, nk = q_rows // tq, k_rows // tk
    qo, ko = q_start // tq, k_start // tk
    width = heads_per_step * LANES
    first = 0
    return pl.pallas_call(
        functools.partial(_diff_kernel, lambda_init=lambda_init),
        grid=(b, DIFF_HEADS // heads_per_step, nq, nk),
        in_specs=[pl.BlockSpec((None, tq, width), lambda bi, h, i, j: (bi, qo + i, first + h)),
                  pl.BlockSpec((None, tk, width), lambda bi, h, i, j: (bi, ko + j, first + h)),
                  pl.BlockSpec((None, tk, width), lambda bi, h, i, j: (bi, ko + j, first + h)),
                  pl.BlockSpec((4, HEAD_DIM), lambda bi, h, i, j: (0, 0)),
                  pl.BlockSpec((1, LANES), lambda bi, h, i, j: (0, 0))],
        out_specs=pl.BlockSpec((None, tq, width), lambda bi, h, i, j: (bi, i, h)),
        out_shape=jax.ShapeDtypeStruct((b, q_rows, DIFF_WIDTH), jnp.bfloat16),
        scratch_shapes=[pltpu.VMEM((2 * heads_per_step, tq, LANES), jnp.float32),
                        pltpu.VMEM((2 * heads_per_step, tq, 2 * LANES), jnp.float32)],
        compiler_params=_cparams(("parallel", "parallel", "parallel", "arbitrary")),
        name="diff_attention",
    )(q, k, v, lam_params, gain)


def _out_proj_kernel(x_ref, c_ref, na_ref, dif_ref, u_ref, up_ref, un_ref, bg_ref, cw_ref, w_ref, mod_ref,
                     *rest, n_lat_tiles, router):
    if router:
        rw_ref, x1_ref, h2_ref, route_ref = rest
    else:
        w1_ref, w3_ref, w2_ref, o_ref = rest
    i = pl.program_id(1)
    tm = x_ref.shape[0]
    first_of_seq = (i == 0) | (i == n_lat_tiles)
    last_of_seq = (i == n_lat_tiles - 1) | (i == n_lat_tiles)
    u = u_ref[...]
    up = jnp.where(first_of_seq, 0.0, up_ref[7:8, :])
    un = jnp.where(last_of_seq, 0.0, un_ref[0:1, :])
    row = lax.broadcasted_iota(jnp.int32, (tm, 1), 0)
    u_prev = jnp.where(row == 0, up, pltpu.roll(u, 1, axis=0))
    u_next = jnp.where(row == tm - 1, un, pltpu.roll(u, tm - 1, axis=0))
    conv = bg_ref[...] * (cw_ref[0:1, :] * u_prev + cw_ref[1:2, :] * u + cw_ref[2:3, :] * u_next)
    o1 = NA_WIDTH
    o2 = NA_WIDTH + DIFF_WIDTH
    mix = (jnp.dot(na_ref[...], w_ref[:o1, :], preferred_element_type=jnp.float32)
           + jnp.dot(dif_ref[...], w_ref[o1:o2, :], preferred_element_type=jnp.float32)
           + jnp.dot(conv.astype(jnp.bfloat16), w_ref[o2:, :], preferred_element_type=jnp.float32))
    x1 = jnp.where(i >= n_lat_tiles, c_ref[...], x_ref[...]) + mod_ref[2:3, :] * mix
    h2 = _rms(x1) * (1.0 + mod_ref[4:5, :]) + mod_ref[3:4, :]
    if not router:
        h = h2.astype(jnp.bfloat16)
        a = jnp.dot(h, w1_ref[...], preferred_element_type=jnp.float32)
        g = jnp.dot(h, w3_ref[...], preferred_element_type=jnp.float32)
        act = (a * jax.nn.sigmoid(a) * g).astype(jnp.bfloat16)
        y = jnp.dot(act, w2_ref[...], preferred_element_type=jnp.float32)
        o_ref[...] = x1 + mod_ref[5:6, :] * y
    else:
        x1_ref[...] = x1
        h2_ref[...] = _pack_halves(h2)
        h_hi = h2.astype(jnp.bfloat16)
        h_lo = (h2 - h_hi.astype(jnp.float32)).astype(jnp.bfloat16)
        logits = (jnp.dot(h_hi, rw_ref[0], preferred_element_type=jnp.float32)
                  + (jnp.dot(h_hi, rw_ref[1], preferred_element_type=jnp.float32)
                     + jnp.dot(h_lo, rw_ref[0], preferred_element_type=jnp.float32)))
        lane = lax.broadcasted_iota(jnp.int32, logits.shape, 1)
        logits = jnp.where(lane < N_EXPERTS, logits, NEG_BIG)
        m1 = jnp.max(logits, axis=-1, keepdims=True)
        i1 = jnp.min(jnp.where(logits == m1, lane, LANES), axis=-1, keepdims=True)
        rest_l = jnp.where(lane == i1, NEG_BIG, logits)
        m2 = jnp.max(rest_l, axis=-1, keepdims=True)
        i2 = jnp.min(jnp.where(rest_l == m2, lane, LANES), axis=-1, keepdims=True)
        e2 = jnp.exp(m2 - m1)
        w1 = 1.0 / (1.0 + e2)
        route_ref[...] = jnp.where(
            lane == 0, i1.astype(jnp.float32),
            jnp.where(lane == 1, i2.astype(jnp.float32),
                      jnp.where(lane == 2, w1, jnp.where(lane == 3, e2 * w1, 0.0))))


def _out_proj(lat, ctx, ctx_block, na, dif, u, bg, conv_w, w_bf16, modtab, n_lat, n_rows,
              router_w=None, ffn=None):
    b, _, d = lat.shape
    tm = TOKEN_TILE
    nlt = n_lat // tm
    sub = 8
    n_sub = u.shape[1] // sub
    tok = lambda width: pl.BlockSpec((None, tm, width), lambda bi, i: (bi, i, 0))
    in_specs = [*_stream_specs(lat, n_lat, ctx_block), tok(NA_WIDTH), tok(DIFF_WIDTH), tok(CONV_WIDTH),
                pl.BlockSpec((None, sub, CONV_WIDTH),
                             lambda bi, i: (bi, jnp.maximum(i * (tm // sub) - 1, 0), 0)),
                pl.BlockSpec((None, sub, CONV_WIDTH),
                             lambda bi, i: (bi, jnp.minimum((i + 1) * (tm // sub), n_sub - 1), 0)),
                tok(CONV_WIDTH),
                pl.BlockSpec((3, CONV_WIDTH), lambda bi, i: (0, 0)),
                pl.BlockSpec((d, d), lambda bi, i: (0, 0)),
                pl.BlockSpec((None, 6, d), lambda bi, i: (bi * 2 + (i >= nlt).astype(jnp.int32), 0, 0))]
    args = [lat, ctx, na, dif, u, u, u, bg, conv_w, w_bf16, modtab]
    assert (router_w is None) != (ffn is None)
    if router_w is None:
        const = lambda w: pl.BlockSpec(w.shape, lambda bi, i: (0, 0), pipeline_mode=pl.Buffered(1))
        in_specs += [const(w) for w in ffn]
        out_specs = tok(d)
        out_shape = jax.ShapeDtypeStruct((b, n_rows, d), jnp.float32)
        args += list(ffn)
    else:
        in_specs.append(pl.BlockSpec((2, d, LANES), lambda bi, i: (0, 0, 0)))
        steps = n_rows // tm
        out_specs = [tok(d), pl.BlockSpec((tm, d // 2), lambda bi, i: (bi * steps + i, 0)), tok(LANES)]
        out_shape = [jax.ShapeDtypeStruct((b, n_rows, d), jnp.float32),
                     jax.ShapeDtypeStruct((b * n_rows, d // 2), jnp.uint32),
                     jax.ShapeDtypeStruct((b, n_rows, LANES), jnp.float32)]
        args.append(router_w)
    return pl.pallas_call(
        functools.partial(_out_proj_kernel, n_lat_tiles=nlt, router=router_w is not None),
        grid=(b, n_rows // tm),
        in_specs=in_specs, out_specs=out_specs, out_shape=out_shape,
        compiler_params=_cparams(("parallel", "parallel")),
        name="out_proj_ffn" if router_w is None else "out_proj_route",
    )(*args)


MOE_TILE = 512
MOE_FF_CHUNK = 896
TOP_K = 2


def _route_plan(eid, n_exp, n_tiles):
    experts = jnp.arange(n_exp, dtype=jnp.int32)
    counts = jnp.sum((eid[:, None] == experts[None, :]).astype(jnp.int32), axis=0)
    tiles = (counts + MOE_TILE - 1) // MOE_TILE
    tile_end = jnp.cumsum(tiles)
    n_used = tile_end[-1]
    j = jnp.arange(n_tiles, dtype=jnp.int32)
    owner = jnp.sum((j[:, None] >= tile_end[None, :]).astype(jnp.int32), axis=1)
    last_owner = jnp.sum((n_used - 1 >= tile_end).astype(jnp.int32))
    tile_expert = jnp.where(j < n_used, owner, last_owner).astype(jnp.int32)
    meta = jnp.concatenate([(tile_end - tiles) * MOE_TILE, counts, tile_end * MOE_TILE,
                            n_used[None]]).astype(jnp.int32)
    return meta, tile_expert


def _slot_kernel(eid_ref, start_ref, tri_ref, sot_ref, wrow_ref, win_ref, next_ref):
    @pl.when(pl.program_id(0) == 0)
    def _():
        next_ref[...] = start_ref[...]

    e_row = eid_ref[...]
    expert = lax.broadcasted_iota(jnp.int32, (next_ref.shape[0], e_row.shape[1]), 0)
    hit = e_row == expert
    onehot = jnp.where(hit, 1.0, 0.0)
    before = jnp.dot(onehot.astype(jnp.bfloat16), tri_ref[...], preferred_element_type=jnp.float32)
    nxt = next_ref[:, 0:1]
    blk = jnp.floor(nxt * (1.0 / TOKEN_TILE))
    slot = before + nxt
    sot_ref[...] = jnp.sum(jnp.where(hit, slot, 0.0), axis=0, keepdims=True).astype(jnp.int32)
    wrow = expert.astype(jnp.float32) * (2 * TOKEN_TILE) + slot - blk * TOKEN_TILE
    wrow_ref[...] = jnp.sum(jnp.where(hit, wrow, 0.0), axis=0, keepdims=True).astype(jnp.int32)
    win_ref[...] = jnp.broadcast_to(blk, win_ref.shape).astype(jnp.int32)
    next_ref[...] = next_ref[...] + jnp.sum(onehot, axis=1, keepdims=True)


def _slots(eid, start, n_exp):
    n_assign = TOKEN_TILE * TOP_K
    steps = eid.shape[0] // n_assign
    assert n_exp == 8, "experts are laid along the eight sublanes"
    tri = jnp.asarray(np.triu(np.ones((n_assign, n_assign), np.float32), k=1), jnp.bfloat16)
    row = pl.BlockSpec((None, 1, n_assign), lambda i: (i, 0, 0))
    return pl.pallas_call(
        _slot_kernel,
        grid=(steps,),
        in_specs=[row, pl.BlockSpec((n_exp, LANES), lambda i: (0, 0)),
                  pl.BlockSpec((n_assign, n_assign), lambda i: (0, 0))],
        out_specs=[row, row, pl.BlockSpec((None, n_exp, LANES), lambda i: (i, 0, 0))],
        out_shape=[jax.ShapeDtypeStruct((steps, 1, n_assign), jnp.int32),
                   jax.ShapeDtypeStruct((steps, 1, n_assign), jnp.int32),
                   jax.ShapeDtypeStruct((steps, n_exp, LANES), jnp.int32)],
        scratch_shapes=[pltpu.VMEM((n_exp, LANES), jnp.float32)],
        compiler_params=_cparams(("arbitrary",)),
        name="moe_slots",
    )(eid.reshape(steps, 1, n_assign),
      jnp.broadcast_to(start.astype(jnp.float32)[:, None], (n_exp, LANES)), tri)


INVERT_CHUNK = 4096


def _invert_kernel(meta_ref, sot_ref, tos_ref, *, n_exp, n_slots):
    i = pl.program_id(0)
    n_assign = sot_ref.shape[1]

    def clear(s, carry):
        tos_ref[s] = 0
        return carry

    @pl.when(i == 0)
    def _():
        for e in range(n_exp):
            lax.fori_loop(meta_ref[e] + meta_ref[n_exp + e], meta_ref[2 * n_exp + e], clear, 0)
        lax.fori_loop(meta_ref[3 * n_exp] * MOE_TILE, n_slots, clear, 0)

    first = i * (n_assign // TOP_K)

    def put(t, carry):
        for k in range(TOP_K):
            tos_ref[sot_ref[0, TOP_K * t + k]] = first + t
        return carry

    lax.fori_loop(0, n_assign // TOP_K, put, 0, unroll=8)


def _invert(meta, sot, n_exp, n_slots):
    steps = sot.size // INVERT_CHUNK
    blocked = pl.BlockSpec((None, 1, INVERT_CHUNK), lambda i, meta: (i, 0, 0), memory_space=pltpu.SMEM)
    return pl.pallas_call(
        functools.partial(_invert_kernel, n_exp=n_exp, n_slots=n_slots),
        grid_spec=pltpu.PrefetchScalarGridSpec(
            num_scalar_prefetch=1, grid=(steps,),
            in_specs=[blocked], out_specs=pl.BlockSpec(memory_space=pltpu.SMEM)),
        out_shape=jax.ShapeDtypeStruct((n_slots,), jnp.int32),
        compiler_params=_cparams(("arbitrary",)),
        name="moe_invert",
    )(meta, sot.reshape(steps, 1, INVERT_CHUNK))


def _experts_kernel(texp_ref, nused_ref, tos_ref, tab_ref, w13_ref, w2_ref, ys_ref,
                    xp_ref, xs_ref, acc_ref):
    j = pl.program_id(0)
    c = pl.program_id(1)
    used = j < nused_ref[0]
    last = c == pl.num_programs(1) - 1

    @pl.when(used & (c == 0))
    def _():
        def gather(s, carry):
            xp_ref[pl.ds(s, 1), :] = tab_ref[pl.ds(tos_ref[0, s], 1), :]
            return carry

        lax.fori_loop(0, MOE_TILE, gather, 0, unroll=8)
        xs_ref[...] = _unpack_halves(xp_ref[...]).astype(xs_ref.dtype)
        acc_ref[...] = jnp.zeros(acc_ref.shape, jnp.float32)

    @pl.when(used)
    def _():
        ag = jnp.dot(xs_ref[...], w13_ref[...], preferred_element_type=jnp.float32)
        fc = ag.shape[1] // 2
        a = ag[:, :fc]
        act = (a * jax.nn.sigmoid(a) * ag[:, fc:]).astype(jnp.bfloat16)
        acc_ref[...] += jnp.dot(act, w2_ref[...].astype(jnp.bfloat16), preferred_element_type=jnp.float32)

    @pl.when(used & last)
    def _():
        ys_ref[...] = _pack_halves(acc_ref[...])

    @pl.when(jnp.logical_not(used) & last)
    def _():
        ys_ref[...] = _pack_halves(jnp.zeros(acc_ref.shape, jnp.float32))


def _experts(tile_expert, n_used, tos, table, w1, w3, w2):
    n_tiles = tos.shape[0]
    n_tok, half = table.shape
    d = 2 * half
    n_exp, _, ff = w1.shape
    fc = MOE_FF_CHUNK
    nc = ff // fc
    assert ff % fc == 0
    w13 = jnp.concatenate([w1.reshape(n_exp, d, nc, fc), w3.reshape(n_exp, d, nc, fc)],
                          axis=-1).reshape(n_exp, d, 2 * ff)
    chunk = lambda j, c, te, nu: jnp.where(j < nu[0], c, nc - 1)
    return pl.pallas_call(
        _experts_kernel,
        grid_spec=pltpu.PrefetchScalarGridSpec(
            num_scalar_prefetch=2, grid=(n_tiles, nc),
            in_specs=[pl.BlockSpec((None, 1, MOE_TILE), lambda j, c, te, nu: (j, 0, 0),
                                   memory_space=pltpu.SMEM),
                      pl.BlockSpec((n_tok, half), lambda j, c, te, nu: (0, 0),
                                   pipeline_mode=pl.Buffered(1)),
                      pl.BlockSpec((None, d, 2 * fc), lambda j, c, te, nu: (te[j], 0, chunk(j, c, te, nu))),
                      pl.BlockSpec((None, fc, d), lambda j, c, te, nu: (te[j], chunk(j, c, te, nu), 0))],
            out_specs=pl.BlockSpec((MOE_TILE, half), lambda j, c, te, nu: (j, 0)),
            scratch_shapes=[pltpu.VMEM((MOE_TILE, half), jnp.uint32),
                            pltpu.VMEM((MOE_TILE, d), jnp.bfloat16),
                            pltpu.VMEM((MOE_TILE, d), jnp.float32)]),
        out_shape=jax.ShapeDtypeStruct((n_tiles * MOE_TILE, half), jnp.uint32),
        compiler_params=_cparams(("arbitrary", "arbitrary")),
        name="moe_experts",
    )(tile_expert, n_used, tos, table, w13, w2)


def _combine_kernel(win_ref, x1_ref, route_ref, wrow_ref, *rest, n_exp):
    ys_refs = rest[:2 * n_exp]
    mod_ref, gain_ref, o_ref, w_ref, r_ref = rest[2 * n_exp:]
    tm = x1_ref.shape[0]
    for blk, ys_ref in enumerate(ys_refs):
        w_ref[pl.ds(blk * tm, tm), :] = ys_ref[...]

    def fetch(r, carry):
        for k in range(TOP_K):
            r_ref[k, pl.ds(r, 1), :] = w_ref[pl.ds(wrow_ref[0, TOP_K * r + k], 1), :]
        return carry

    lax.fori_loop(0, tm, fetch, 0, unroll=8)
    route = route_ref[...]
    y = route[:, 2:3] * _unpack_halves(r_ref[0]) + route[:, 3:4] * _unpack_halves(r_ref[1])
    x2 = x1_ref[...] + mod_ref[5:6, :] * y
    o_ref[...] = _rms(x2) * gain_ref[...]


def _combine_final(win, x1, route, wrow3, ys, modtab, final_gain, tiles_per_batch, n_exp):
    n_tok, d = x1.shape
    tm = TOKEN_TILE
    n_assign = tm * TOP_K
    ys_specs = [pl.BlockSpec((tm, d // 2), lambda i, win, e=e, jj=jj: (win[i * n_exp + e] + jj, 0))
                for e in range(n_exp) for jj in range(2)]
    smem_blk = pl.BlockSpec((None, 1, n_assign), lambda i, win: (i, 0, 0), memory_space=pltpu.SMEM)
    return pl.pallas_call(
        functools.partial(_combine_kernel, n_exp=n_exp),
        grid_spec=pltpu.PrefetchScalarGridSpec(
            num_scalar_prefetch=1, grid=(n_tok // tm,),
            in_specs=[pl.BlockSpec((tm, d), lambda i, win: (i, 0)),
                      pl.BlockSpec((tm, LANES), lambda i, win: (i, 0)),
                      smem_blk, *ys_specs,
                      pl.BlockSpec((None, 6, d), lambda i, win: ((i // tiles_per_batch) * 2, 0, 0)),
                      pl.BlockSpec((1, d), lambda i, win: (0, 0))],
            out_specs=pl.BlockSpec((tm, d), lambda i, win: (i, 0)),
            scratch_shapes=[pltpu.VMEM((2 * n_exp * tm, d // 2), jnp.uint32),
                            pltpu.VMEM((TOP_K, tm, d // 2), jnp.uint32)]),
        out_shape=jax.ShapeDtypeStruct((n_tok, d), jnp.float32),
        compiler_params=_cparams(("arbitrary",)),
        name="moe_combine",
    )(win, x1, route, wrow3, *([ys] * (2 * n_exp)), modtab, final_gain.reshape(1, d))


def _moe_ffn_final(x1, h2p, route, w1, w3, w2, modtab, final_gain):
    b, n, d = x1.shape
    n_tok = b * n
    n_exp = w1.shape[0]
    n_tiles = n_tok * TOP_K // MOE_TILE + n_exp + 1
    eid = route[..., :TOP_K].astype(jnp.int32).reshape(n_tok * TOP_K)
    meta, tile_expert = _route_plan(eid, n_exp, n_tiles)
    sot3, wrow3, win = _slots(eid, meta[:n_exp], n_exp)
    tos = _invert(meta, sot3, n_exp, n_tiles * MOE_TILE)
    ys = _experts(tile_expert, meta[3 * n_exp:], tos.reshape(n_tiles, 1, MOE_TILE),
                  h2p, w1, w3, w2)
    out = _combine_final(win[:, :, 0].reshape(-1), x1.reshape(n_tok, d), route.reshape(n_tok, LANES),
                         wrow3, ys, modtab, final_gain, n // TOKEN_TILE, n_exp)
    return out.reshape(b, n, d)


def _rope_tables(n_lat, n_ctx):
    t = jnp.arange(n_lat, dtype=jnp.int32)
    n_freq = HEAD_DIM // 4
    inv_freq = ROPE_BASE ** (-jnp.arange(n_freq, dtype=jnp.float32) / n_freq)
    ang = jnp.concatenate([(t // GRID_W).astype(jnp.float32)[:, None] * inv_freq,
                           (t % GRID_W).astype(jnp.float32)[:, None] * inv_freq], axis=-1)
    cos, sin = jnp.cos(ang), jnp.sin(ang)
    reps = LANES // HEAD_DIM
    cos_t = jnp.tile(jnp.concatenate([cos, cos], axis=-1), (1, reps))
    sin_t = jnp.tile(jnp.concatenate([-sin, sin], axis=-1), (1, reps))
    cos_t = jnp.concatenate([cos_t, jnp.ones((n_ctx, LANES), jnp.float32)], axis=0)
    sin_t = jnp.concatenate([sin_t, jnp.zeros((n_ctx, LANES), jnp.float32)], axis=0)
    return cos_t, sin_t


def kernel(x, c, ctx, c_ctx, ada_w, ada_b, w_in, w_out, na_rpb, diff_lambda, diff_subln, conv_w,
           ffn_w1, ffn_w3, ffn_w2, router_w, moe_w1, moe_w3, moe_w2, final_gain):
    b, n, d = x.shape
    n_ctx = ctx.shape[1]
    depth = w_in.shape[0]
    assert d == D_MODEL and n % TOKEN_TILE == 0 and n_ctx == TOKEN_TILE and b + 1 <= 8
    assert depth == 2, "layer 0 dense with a context stream, layer 1 routed and final"
    bf = jnp.bfloat16

    stream = (x, ctx, 0)
    cvec = jnp.zeros((8, d), jnp.float32).at[:b].set(c).at[b].set(c_ctx)
    mod = _modulation(cvec, ada_w, ada_b).reshape(depth, 8, 6, d)
    cos_t, sin_t = _rope_tables(n, n_ctx)
    rows = n // GRID_W

    out = None
    for i in range(depth):
        lambda_init = 0.8 - 0.6 * math.exp(-0.3 * i)
        ctx_out = i < depth - 1
        modtab = jnp.stack([mod[i, :b], jnp.broadcast_to(mod[i, b], (b, 6, d))], axis=1).reshape(2 * b, 6, d)
        q, k, v, u, bg = _in_proj(*stream, modtab, cos_t, sin_t, w_in[i].astype(bf), n)
        bias = _na_bias_table(na_rpb[i], rows)
        na = _na_latent(q, k, v, bias, n)
        gain = diff_subln[i].reshape(1, 2 * HEAD_DIM)
        dif = _diff_attention(q, k, v, diff_lambda[i], gain, lambda_init,
                              q_rows=n, q_start=0, tq=1024, k_rows=n + n_ctx, k_start=0, tk=1408,
                              heads_per_step=4)
        if ctx_out:
            na_c = _na_context(q, k, v, n)
            dif_c = _diff_attention(q, k, v, diff_lambda[i], gain, lambda_init,
                                    q_rows=n_ctx, q_start=n, tq=n_ctx, k_rows=n_ctx, k_start=n, tk=n_ctx,
                                    heads_per_step=4)
            na = jnp.concatenate([na, na_c], axis=1)
            dif = jnp.concatenate([dif, dif_c], axis=1)
            m = i // 2
            x_all = _out_proj(*stream, na, dif, u, bg, conv_w[i], w_out[i].astype(bf), modtab, n, n + n_ctx,
                              ffn=(ffn_w1[m].astype(bf), ffn_w3[m].astype(bf), ffn_w2[m].astype(bf)))
            stream = (x_all, x_all, n // TOKEN_TILE)
        else:
            m = i // 2
            rw = jnp.zeros((d, LANES), jnp.float32).at[:, :N_EXPERTS].set(router_w[m])
            rw_hi = rw.astype(bf)
            rw = jnp.stack([rw_hi, (rw - rw_hi.astype(jnp.float32)).astype(bf)])
            x1, h2p, route = _out_proj(*stream, na, dif, u, bg, conv_w[i], w_out[i].astype(bf), modtab, n, n,
                                       router_w=rw)
            out = _moe_ffn_final(x1, h2p, route, moe_w1[m].astype(bf), moe_w3[m].astype(bf),
                                 moe_w2[m], modtab, final_gain)
    return out
```

```python
import functools
import math

import numpy as np
import jax
import jax.numpy as jnp
from jax import lax
from jax.experimental import pallas as pl
from jax.experimental.pallas import tpu as pltpu

D_MODEL = 1024
GRID_W = 64
HEAD_DIM = 64
NA_HEADS = 4
NA_WIDTH = NA_HEADS * HEAD_DIM
WIN_H = 8
WIN_W = 16
DIFF_HEADS = 4
DIFF_WIDTH = DIFF_HEADS * 2 * HEAD_DIM
CONV_WIDTH = 256
QKV_COLS = NA_WIDTH + DIFF_WIDTH
NA_COL_BLOCK = DIFF_WIDTH // NA_WIDTH
IN_COLS = 3 * QKV_COLS + 3 * CONV_WIDTH
ROPE_BASE = 10000.0
N_EXPERTS = 8
EPS = 1e-6
NEG_BIG = -1e30
LOG2E = math.log2(math.e)

LANES = 128
TOKEN_TILE = 256
NA_ROWS = 4
NA_KEY_ROWS = NA_ROWS + WIN_H
VMEM_LIMIT = 56 * 1024 * 1024


def _cparams(sem):
    return pltpu.CompilerParams(dimension_semantics=sem, vmem_limit_bytes=VMEM_LIMIT)


def _rms(x):
    return x * lax.rsqrt(jnp.mean(x * x, axis=-1, keepdims=True) + EPS)


def _pack_halves(x):
    half = x.shape[1] // 2
    return pltpu.pack_elementwise([x[:, :half], x[:, half:]], packed_dtype=jnp.bfloat16)


def _unpack_halves(p):
    parts = [pltpu.unpack_elementwise(p, index=i, packed_dtype=jnp.bfloat16, unpacked_dtype=jnp.float32)
             for i in range(2)]
    return jnp.concatenate(parts, axis=1)


def _mod_kernel(c_ref, w_ref, b_ref, o_ref):
    cv = c_ref[...]
    s = cv * jax.nn.sigmoid(cv)
    o_ref[...] = jnp.dot(s, w_ref[...], preferred_element_type=jnp.float32,
                         precision=lax.Precision.HIGHEST) + b_ref[...]


def _modulation(cvec, ada_w, ada_b):
    depth, d, cols = ada_w.shape
    cb = 1536
    return pl.pallas_call(
        _mod_kernel,
        grid=(depth, cols // cb),
        in_specs=[pl.BlockSpec((8, d), lambda i, j: (0, 0)),
                  pl.BlockSpec((None, d, cb), lambda i, j: (i, 0, j)),
                  pl.BlockSpec((None, 1, cb), lambda i, j: (i, 0, j))],
        out_specs=pl.BlockSpec((None, 8, cb), lambda i, j: (i, 0, j)),
        out_shape=jax.ShapeDtypeStruct((depth, 8, cols), jnp.float32),
        compiler_params=_cparams(("parallel", "parallel")),
        name="modulation",
    )(cvec, ada_w, ada_b.reshape(depth, 1, cols))


def _rope(z, cos, sin_signed):
    width = z.shape[1]
    reps = width // LANES
    c = jnp.concatenate([cos] * reps, axis=1)
    s = jnp.concatenate([sin_signed] * reps, axis=1)
    lane = lax.broadcasted_iota(jnp.int32, (1, width), 1)
    first_half = (lane % HEAD_DIM) < (HEAD_DIM // 2)
    swapped = jnp.where(first_half,
                        pltpu.roll(z, width - HEAD_DIM // 2, axis=1),
                        pltpu.roll(z, HEAD_DIM // 2, axis=1))
    return z * c + swapped * s


def _stream_specs(lat, n_lat, ctx_block):
    tm = TOKEN_TILE
    nlt = n_lat // tm
    d = lat.shape[-1]
    return [pl.BlockSpec((None, tm, d), lambda bi, i: (bi, jnp.minimum(i, nlt - 1), 0)),
            pl.BlockSpec((None, tm, d), lambda bi, i: (bi, ctx_block, 0))]


def _in_proj_kernel(x_ref, c_ref, mod_ref, cos_ref, sin_ref, w_ref, q_ref, k_ref, v_ref, u_ref, bg_ref,
                    *, n_lat_tiles):
    x = jnp.where(pl.program_id(1) >= n_lat_tiles, c_ref[...], x_ref[...])
    h = _rms(x) * (1.0 + mod_ref[1:2, :]) + mod_ref[0:1, :]
    p = jnp.dot(h.astype(jnp.bfloat16), w_ref[...], preferred_element_type=jnp.float32)
    cos = cos_ref[...]
    sin = sin_ref[...]
    scale = HEAD_DIM ** -0.5 * LOG2E
    q_ref[:, :DIFF_WIDTH] = (_rope(p[:, NA_WIDTH:QKV_COLS], cos, sin) * scale).astype(q_ref.dtype)
    q_ref[:, DIFF_WIDTH:] = (p[:, :NA_WIDTH] * scale).astype(q_ref.dtype)
    k_ref[:, :DIFF_WIDTH] = _rope(p[:, QKV_COLS + NA_WIDTH:2 * QKV_COLS], cos, sin).astype(k_ref.dtype)
    k_ref[:, DIFF_WIDTH:] = p[:, QKV_COLS:QKV_COLS + NA_WIDTH].astype(k_ref.dtype)
    o = 2 * QKV_COLS
    v_ref[:, :DIFF_WIDTH] = p[:, o + NA_WIDTH:o + QKV_COLS].astype(v_ref.dtype)
    v_ref[:, DIFF_WIDTH:] = p[:, o:o + NA_WIDTH].astype(v_ref.dtype)
    o = 3 * QKV_COLS
    xin = p[:, o:o + CONV_WIDTH]
    bg_ref[...] = p[:, o + CONV_WIDTH:o + 2 * CONV_WIDTH]
    u_ref[...] = p[:, o + 2 * CONV_WIDTH:o + 3 * CONV_WIDTH] * xin


def _in_proj(lat, ctx, ctx_block, modtab, cos_t, sin_t, w_bf16, n_lat):
    b, _, d = lat.shape
    tm = TOKEN_TILE
    nlt = n_lat // tm
    t = n_lat + tm
    tok = lambda width: pl.BlockSpec((None, tm, width), lambda bi, i: (bi, i, 0))
    return pl.pallas_call(
        functools.partial(_in_proj_kernel, n_lat_tiles=nlt),
        grid=(b, t // tm),
        in_specs=[*_stream_specs(lat, n_lat, ctx_block),
                  pl.BlockSpec((None, 6, d), lambda bi, i: (bi * 2 + (i >= nlt).astype(jnp.int32), 0, 0)),
                  pl.BlockSpec((tm, LANES), lambda bi, i: (i, 0)),
                  pl.BlockSpec((tm, LANES), lambda bi, i: (i, 0)),
                  pl.BlockSpec((d, IN_COLS), lambda bi, i: (0, 0))],
        out_specs=[tok(QKV_COLS), tok(QKV_COLS), tok(QKV_COLS), tok(CONV_WIDTH), tok(CONV_WIDTH)],
        out_shape=[jax.ShapeDtypeStruct((b, t, QKV_COLS), jnp.bfloat16)] * 3
        + [jax.ShapeDtypeStruct((b, t, CONV_WIDTH), jnp.float32)] * 2,
        compiler_params=_cparams(("parallel", "parallel")),
        name="in_proj",
    )(lat, ctx, modtab, cos_t, sin_t, w_bf16)


def _na_kernel(*refs, n_lat_blocks):
    q_ref = refs[0]
    k_refs = refs[1:2 + n_lat_blocks]
    v_refs = refs[2 + n_lat_blocks:3 + 2 * n_lat_blocks]
    if n_lat_blocks:
        bias_ref, o_ref = refs[3 + 2 * n_lat_blocks:]
    else:
        (o_ref,) = refs[3 + 2 * n_lat_blocks:]
    lane = lax.broadcasted_iota(jnp.int32, (1, LANES), 1)
    lat_keys = n_lat_blocks * TOKEN_TILE
    for pair in range(NA_HEADS // 2):
        cols = slice(pair * LANES, (pair + 1) * LANES)
        q = q_ref[:, cols]
        outs = []
        for hh in range(2):
            head = 2 * pair + hh
            in_head = (lane >= hh * HEAD_DIM) & (lane < (hh + 1) * HEAD_DIM)
            qm = jnp.where(in_head, q, jnp.zeros_like(q))
            s = jnp.concatenate(
                [lax.dot_general(qm, kr[:, cols], (((1,), (1,)), ((), ())),
                                 preferred_element_type=jnp.float32) for kr in k_refs], axis=1)
            if n_lat_blocks:
                s = jnp.concatenate([s[:, :lat_keys] + bias_ref[head], s[:, lat_keys:]], axis=1)
            m = jnp.max(s, axis=-1, keepdims=True)
            eb = jnp.exp2(s - m).astype(jnp.bfloat16)
            acc = None
            for j, vr in enumerate(v_refs):
                v = vr[:, cols]
                part = jnp.dot(eb[:, j * TOKEN_TILE:(j + 1) * TOKEN_TILE],
                               jnp.concatenate([v, jnp.ones_like(v)], axis=1),
                               preferred_element_type=jnp.float32)
                acc = part if acc is None else acc + part
            outs.append(acc[:, :LANES] / acc[:, LANES:])
        o_ref[:, cols] = jnp.where(lane < HEAD_DIM, outs[0], outs[1]).astype(o_ref.dtype)


def _na_bias_table(rpb, rows):
    n_groups = rows // NA_ROWS
    heads = rpb.shape[0]
    padded = jnp.pad(rpb.astype(jnp.float32) * LOG2E, ((0, 0), (0, 0), (GRID_W, GRID_W)))
    toeplitz = jnp.stack([padded[:, :, GRID_W + WIN_W - 1 - qc:2 * GRID_W + WIN_W - 1 - qc]
                          for qc in range(GRID_W)], axis=2)
    qc = np.arange(GRID_W)[:, None]
    kc = np.arange(GRID_W)[None, :]
    col_start = np.clip(qc - WIN_W // 2, 0, GRID_W - WIN_W)
    col_valid = (kc >= col_start) & (kc < col_start + WIN_W)
    toeplitz = jnp.where(col_valid[None, None], toeplitz, NEG_BIG)
    masked = jnp.full((heads, GRID_W, GRID_W), NEG_BIG, jnp.float32)
    tables = []
    for g in (0, 1, n_groups - 1):
        ws = min(max(g - 1, 0), n_groups - 3) * NA_ROWS
        q_rows = []
        for qr in range(g * NA_ROWS, (g + 1) * NA_ROWS):
            row_start = min(max(qr - WIN_H // 2, 0), rows - WIN_H)
            blocks = [toeplitz[:, kr - qr + WIN_H - 1] if row_start <= kr < row_start + WIN_H else masked
                      for kr in range(ws, ws + NA_KEY_ROWS)]
            q_rows.append(jnp.concatenate(blocks, axis=-1))
        tables.append(jnp.concatenate(q_rows, axis=1))
    return jnp.stack(tables)


def _na_latent(q, k, v, bias, n_lat):
    b = q.shape[0]
    tm = TOKEN_TILE
    assert NA_ROWS * GRID_W == tm and NA_KEY_ROWS * GRID_W == 3 * tm
    ng = n_lat // tm
    ctx_blk = n_lat // tm

    def kv_spec(j):
        return pl.BlockSpec((None, tm, NA_WIDTH),
                            lambda bi, g: (bi, jnp.clip(g - 1, 0, ng - 3) + j, NA_COL_BLOCK))

    ctx_spec = pl.BlockSpec((None, tm, NA_WIDTH), lambda bi, g: (bi, ctx_blk, NA_COL_BLOCK))
    variant = lambda bi, g: ((g > 0).astype(jnp.int32) + (g == ng - 1).astype(jnp.int32), 0, 0, 0)
    return pl.pallas_call(
        functools.partial(_na_kernel, n_lat_blocks=3),
        grid=(b, ng),
        in_specs=[pl.BlockSpec((None, tm, NA_WIDTH), lambda bi, g: (bi, g, NA_COL_BLOCK)),
                  kv_spec(0), kv_spec(1), kv_spec(2), ctx_spec,
                  kv_spec(0), kv_spec(1), kv_spec(2), ctx_spec,
                  pl.BlockSpec((None, NA_HEADS, tm, 3 * tm), variant)],
        out_specs=pl.BlockSpec((None, tm, NA_WIDTH), lambda bi, g: (bi, g, 0)),
        out_shape=jax.ShapeDtypeStruct((b, n_lat, NA_WIDTH), jnp.bfloat16),
        compiler_params=_cparams(("parallel", "parallel")),
        name="na_latent",
    )(q, k, k, k, k, v, v, v, v, bias)


def _na_context(q, k, v, n_lat):
    b = q.shape[0]
    tm = TOKEN_TILE
    ctx_spec = pl.BlockSpec((None, tm, NA_WIDTH), lambda bi: (bi, n_lat // tm, NA_COL_BLOCK))
    return pl.pallas_call(
        functools.partial(_na_kernel, n_lat_blocks=0),
        grid=(b,),
        in_specs=[ctx_spec, ctx_spec, ctx_spec],
        out_specs=pl.BlockSpec((None, tm, NA_WIDTH), lambda bi: (bi, 0, 0)),
        out_shape=jax.ShapeDtypeStruct((b, tm, NA_WIDTH), jnp.bfloat16),
        compiler_params=_cparams(("parallel",)),
        name="na_context",
    )(q, k, v)


def _diff_kernel(q_ref, k_ref, v_ref, lam_ref, gain_ref, o_ref, m_ref, acc_ref, *, lambda_init):
    kk = pl.program_id(3)

    @pl.when(kk == 0)
    def _():
        m_ref[...] = jnp.full(m_ref.shape, NEG_BIG, jnp.float32)
        acc_ref[...] = jnp.zeros(acc_ref.shape, jnp.float32)

    lane = lax.broadcasted_iota(jnp.int32, (1, LANES), 1)
    reps = k_ref.shape[0] // LANES
    n_heads = q_ref.shape[1] // LANES
    for vh in range(n_heads):
        cols = slice(vh * LANES, (vh + 1) * LANES)
        q = q_ref[:, cols]
        k = k_ref[:, cols]
        v = v_ref[:, cols]
        v_ext = jnp.concatenate([v, jnp.ones_like(v)], axis=1)
        for hh in range(2):
            idx = 2 * vh + hh
            in_head = (lane >= hh * HEAD_DIM) & (lane < (hh + 1) * HEAD_DIM)
            qm = jnp.where(in_head, q, jnp.zeros_like(q))
            s = lax.dot_general(qm, k, (((1,), (1,)), ((), ())), preferred_element_type=jnp.float32)
            m_old = m_ref[idx]
            m_new = jnp.maximum(m_old, jnp.max(s, axis=-1, keepdims=True))
            alpha = jnp.exp2(m_old - m_new)
            p = jnp.exp2(s - jnp.concatenate([m_new] * reps, axis=1)).astype(jnp.bfloat16)
            pv = jnp.dot(p, v_ext, preferred_element_type=jnp.float32)
            acc_ref[idx] = jnp.concatenate([alpha, alpha], axis=1) * acc_ref[idx] + pv
            m_ref[idx] = m_new

    @pl.when(kk == pl.num_programs(3) - 1)
    def _():
        lp = lam_ref[...]
        lam = (jnp.exp(jnp.sum(lp[0:1] * lp[1:2], axis=-1, keepdims=True))
               - jnp.exp(jnp.sum(lp[2:3] * lp[3:4], axis=-1, keepdims=True)) + lambda_init)
        for vh in range(n_heads):
            a0 = acc_ref[2 * vh]
            a1 = acc_ref[2 * vh + 1]
            o = a0[:, :LANES] / a0[:, LANES:] - lam * (a1[:, :LANES] / a1[:, LANES:])
            o_ref[:, vh * LANES:(vh + 1) * LANES] = (
                _rms(o) * gain_ref[...] * (1.0 - lambda_init)).astype(o_ref.dtype)


def _diff_attention(q, k, v, lam_params, gain, lambda_init, *, q_rows, q_start, tq, k_rows, k_start, tk,
                    heads_per_step):
    b = q.shape[0]
    nq, nk = q_rows // tq, k_rows // tk
    qo, ko = q_start // tq, k_start // tk
    width = heads_per_step * LANES
    first = 0
    return pl.pallas_call(
        functools.partial(_diff_kernel, lambda_init=lambda_init),
        grid=(b, DIFF_HEADS // heads_per_step, nq, nk),
        in_specs=[pl.BlockSpec((None, tq, width), lambda bi, h, i, j: (bi, qo + i, first + h)),
                  pl.BlockSpec((None, tk, width), lambda bi, h, i, j: (bi, ko + j, first + h)),
                  pl.BlockSpec((None, tk, width), lambda bi, h, i, j: (bi, ko + j, first + h)),
                  pl.BlockSpec((4, HEAD_DIM), lambda bi, h, i, j: (0, 0)),
                  pl.BlockSpec((1, LANES), lambda bi, h, i, j: (0, 0))],
        out_specs=pl.BlockSpec((None, tq, width), lambda bi, h, i, j: (bi, i, h)),
        out_shape=jax.ShapeDtypeStruct((b, q_rows, DIFF_WIDTH), jnp.bfloat16),
        scratch_shapes=[pltpu.VMEM((2 * heads_per_step, tq, LANES), jnp.float32),
                        pltpu.VMEM((2 * heads_per_step, tq, 2 * LANES), jnp.float32)],
        compiler_params=_cparams(("parallel", "parallel", "parallel", "arbitrary")),
        name="diff_attention",
    )(q, k, v, lam_params, gain)


def _out_proj_kernel(x_ref, c_ref, na_ref, dif_ref, u_ref, up_ref, un_ref, bg_ref, cw_ref, w_ref, mod_ref,
                     *rest, n_lat_tiles, router):
    if router:
        rw_ref, x1_ref, h2_ref, route_ref = rest
    else:
        w1_ref, w3_ref, w2_ref, o_ref = rest
    i = pl.program_id(1)
    tm = x_ref.shape[0]
    first_of_seq = (i == 0) | (i == n_lat_tiles)
    last_of_seq = (i == n_lat_tiles - 1) | (i == n_lat_tiles)
    u = u_ref[...]
    up = jnp.where(first_of_seq, 0.0, up_ref[7:8, :])
    un = jnp.where(last_of_seq, 0.0, un_ref[0:1, :])
    row = lax.broadcasted_iota(jnp.int32, (tm, 1), 0)
    u_prev = jnp.where(row == 0, up, pltpu.roll(u, 1, axis=0))
    u_next = jnp.where(row == tm - 1, un, pltpu.roll(u, tm - 1, axis=0))
    conv = bg_ref[...] * (cw_ref[0:1, :] * u_prev + cw_ref[1:2, :] * u + cw_ref[2:3, :] * u_next)
    o1 = NA_WIDTH
    o2 = NA_WIDTH + DIFF_WIDTH
    mix = (jnp.dot(na_ref[...], w_ref[:o1, :], preferred_element_type=jnp.float32)
           + jnp.dot(dif_ref[...], w_ref[o1:o2, :], preferred_element_type=jnp.float32)
           + jnp.dot(conv.astype(jnp.bfloat16), w_ref[o2:, :], preferred_element_type=jnp.float32))
    x1 = jnp.where(i >= n_lat_tiles, c_ref[...], x_ref[...]) + mod_ref[2:3, :] * mix
    h2 = _rms(x1) * (1.0 + mod_ref[4:5, :]) + mod_ref[3:4, :]
    if not router:
        h = h2.astype(jnp.bfloat16)
        a = jnp.dot(h, w1_ref[...], preferred_element_type=jnp.float32)
        g = jnp.dot(h, w3_ref[...], preferred_element_type=jnp.float32)
        act = (a * jax.nn.sigmoid(a) * g).astype(jnp.bfloat16)
        y = jnp.dot(act, w2_ref[...], preferred_element_type=jnp.float32)
        o_ref[...] = x1 + mod_ref[5:6, :] * y
    else:
        x1_ref[...] = x1
        h2_ref[...] = _pack_halves(h2)
        h_hi = h2.astype(jnp.bfloat16)
        h_lo = (h2 - h_hi.astype(jnp.float32)).astype(jnp.bfloat16)
        logits = (jnp.dot(h_hi, rw_ref[0], preferred_element_type=jnp.float32)
                  + (jnp.dot(h_hi, rw_ref[1], preferred_element_type=jnp.float32)
                     + jnp.dot(h_lo, rw_ref[0], preferred_element_type=jnp.float32)))
        lane = lax.broadcasted_iota(jnp.int32, logits.shape, 1)
        logits = jnp.where(lane < N_EXPERTS, logits, NEG_BIG)
        m1 = jnp.max(logits, axis=-1, keepdims=True)
        i1 = jnp.min(jnp.where(logits == m1, lane, LANES), axis=-1, keepdims=True)
        rest_l = jnp.where(lane == i1, NEG_BIG, logits)
        m2 = jnp.max(rest_l, axis=-1, keepdims=True)
        i2 = jnp.min(jnp.where(rest_l == m2, lane, LANES), axis=-1, keepdims=True)
        e2 = jnp.exp(m2 - m1)
        w1 = 1.0 / (1.0 + e2)
        route_ref[...] = jnp.where(
            lane == 0, i1.astype(jnp.float32),
            jnp.where(lane == 1, i2.astype(jnp.float32),
                      jnp.where(lane == 2, w1, jnp.where(lane == 3, e2 * w1, 0.0))))


def _out_proj(lat, ctx, ctx_block, na, dif, u, bg, conv_w, w_bf16, modtab, n_lat, n_rows,
              router_w=None, ffn=None):
    b, _, d = lat.shape
    tm = TOKEN_TILE
    nlt = n_lat // tm
    sub = 8
    n_sub = u.shape[1] // sub
    tok = lambda width: pl.BlockSpec((None, tm, width), lambda bi, i: (bi, i, 0))
    in_specs = [*_stream_specs(lat, n_lat, ctx_block), tok(NA_WIDTH), tok(DIFF_WIDTH), tok(CONV_WIDTH),
                pl.BlockSpec((None, sub, CONV_WIDTH),
                             lambda bi, i: (bi, jnp.maximum(i * (tm // sub) - 1, 0), 0)),
                pl.BlockSpec((None, sub, CONV_WIDTH),
                             lambda bi, i: (bi, jnp.minimum((i + 1) * (tm // sub), n_sub - 1), 0)),
                tok(CONV_WIDTH),
                pl.BlockSpec((3, CONV_WIDTH), lambda bi, i: (0, 0)),
                pl.BlockSpec((d, d), lambda bi, i: (0, 0)),
                pl.BlockSpec((None, 6, d), lambda bi, i: (bi * 2 + (i >= nlt).astype(jnp.int32), 0, 0))]
    args = [lat, ctx, na, dif, u, u, u, bg, conv_w, w_bf16, modtab]
    assert (router_w is None) != (ffn is None)
    if router_w is None:
        const = lambda w: pl.BlockSpec(w.shape, lambda bi, i: (0, 0), pipeline_mode=pl.Buffered(1))
        in_specs += [const(w) for w in ffn]
        out_specs = tok(d)
        out_shape = jax.ShapeDtypeStruct((b, n_rows, d), jnp.float32)
        args += list(ffn)
    else:
        in_specs.append(pl.BlockSpec((2, d, LANES), lambda bi, i: (0, 0, 0)))
        steps = n_rows // tm
        out_specs = [tok(d), pl.BlockSpec((tm, d // 2), lambda bi, i: (bi * steps + i, 0)), tok(LANES)]
        out_shape = [jax.ShapeDtypeStruct((b, n_rows, d), jnp.float32),
                     jax.ShapeDtypeStruct((b * n_rows, d // 2), jnp.uint32),
                     jax.ShapeDtypeStruct((b, n_rows, LANES), jnp.float32)]
        args.append(router_w)
    return pl.pallas_call(
        functools.partial(_out_proj_kernel, n_lat_tiles=nlt, router=router_w is not None),
        grid=(b, n_rows // tm),
        in_specs=in_specs, out_specs=out_specs, out_shape=out_shape,
        compiler_params=_cparams(("parallel", "parallel")),
        name="out_proj_ffn" if router_w is None else "out_proj_route",
    )(*args)


MOE_TILE = 512
MOE_FF_CHUNK = 896
TOP_K = 2


def _route_plan(eid, n_exp, n_tiles):
    experts = jnp.arange(n_exp, dtype=jnp.int32)
    counts = jnp.sum((eid[:, None] == experts[None, :]).astype(jnp.int32), axis=0)
    tiles = (counts + MOE_TILE - 1) // MOE_TILE
    tile_end = jnp.cumsum(tiles)
    n_used = tile_end[-1]
    j = jnp.arange(n_tiles, dtype=jnp.int32)
    owner = jnp.sum((j[:, None] >= tile_end[None, :]).astype(jnp.int32), axis=1)
    last_owner = jnp.sum((n_used - 1 >= tile_end).astype(jnp.int32))
    tile_expert = jnp.where(j < n_used, owner, last_owner).astype(jnp.int32)
    meta = jnp.concatenate([(tile_end - tiles) * MOE_TILE, counts, tile_end * MOE_TILE,
                            n_used[None]]).astype(jnp.int32)
    return meta, tile_expert


def _slot_kernel(eid_ref, start_ref, tri_ref, sot_ref, wrow_ref, win_ref, next_ref):
    @pl.when(pl.program_id(0) == 0)
    def _():
        next_ref[...] = start_ref[...]

    e_row = eid_ref[...]
    expert = lax.broadcasted_iota(jnp.int32, (next_ref.shape[0], e_row.shape[1]), 0)
    hit = e_row == expert
    onehot = jnp.where(hit, 1.0, 0.0)
    before = jnp.dot(onehot.astype(jnp.bfloat16), tri_ref[...], preferred_element_type=jnp.float32)
    nxt = next_ref[:, 0:1]
    blk = jnp.floor(nxt * (1.0 / TOKEN_TILE))
    slot = before + nxt
    sot_ref[...] = jnp.sum(jnp.where(hit, slot, 0.0), axis=0, keepdims=True).astype(jnp.int32)
    wrow = expert.astype(jnp.float32) * (2 * TOKEN_TILE) + slot - blk * TOKEN_TILE
    wrow_ref[...] = jnp.sum(jnp.where(hit, wrow, 0.0), axis=0, keepdims=True).astype(jnp.int32)
    win_ref[...] = jnp.broadcast_to(blk, win_ref.shape).astype(jnp.int32)
    next_ref[...] = next_ref[...] + jnp.sum(onehot, axis=1, keepdims=True)


def _slots(eid, start, n_exp):
    n_assign = TOKEN_TILE * TOP_K
    steps = eid.shape[0] // n_assign
    assert n_exp == 8, "experts are laid along the eight sublanes"
    tri = jnp.asarray(np.triu(np.ones((n_assign, n_assign), np.float32), k=1), jnp.bfloat16)
    row = pl.BlockSpec((None, 1, n_assign), lambda i: (i, 0, 0))
    return pl.pallas_call(
        _slot_kernel,
        grid=(steps,),
        in_specs=[row, pl.BlockSpec((n_exp, LANES), lambda i: (0, 0)),
                  pl.BlockSpec((n_assign, n_assign), lambda i: (0, 0))],
        out_specs=[row, row, pl.BlockSpec((None, n_exp, LANES), lambda i: (i, 0, 0))],
        out_shape=[jax.ShapeDtypeStruct((steps, 1, n_assign), jnp.int32),
                   jax.ShapeDtypeStruct((steps, 1, n_assign), jnp.int32),
                   jax.ShapeDtypeStruct((steps, n_exp, LANES), jnp.int32)],
        scratch_shapes=[pltpu.VMEM((n_exp, LANES), jnp.float32)],
        compiler_params=_cparams(("arbitrary",)),
        name="moe_slots",
    )(eid.reshape(steps, 1, n_assign),
      jnp.broadcast_to(start.astype(jnp.float32)[:, None], (n_exp, LANES)), tri)


INVERT_CHUNK = 4096


def _invert_kernel(meta_ref, sot_ref, tos_ref, *, n_exp, n_slots):
    i = pl.program_id(0)
    n_assign = sot_ref.shape[1]

    def clear(s, carry):
        tos_ref[s] = 0
        return carry

    @pl.when(i == 0)
    def _():
        for e in range(n_exp):
            lax.fori_loop(meta_ref[e] + meta_ref[n_exp + e], meta_ref[2 * n_exp + e], clear, 0)
        lax.fori_loop(meta_ref[3 * n_exp] * MOE_TILE, n_slots, clear, 0)

    first = i * (n_assign // TOP_K)

    def put(t, carry):
        for k in range(TOP_K):
            tos_ref[sot_ref[0, TOP_K * t + k]] = first + t
        return carry

    lax.fori_loop(0, n_assign // TOP_K, put, 0, unroll=8)


def _invert(meta, sot, n_exp, n_slots):
    steps = sot.size // INVERT_CHUNK
    blocked = pl.BlockSpec((None, 1, INVERT_CHUNK), lambda i, meta: (i, 0, 0), memory_space=pltpu.SMEM)
    return pl.pallas_call(
        functools.partial(_invert_kernel, n_exp=n_exp, n_slots=n_slots),
        grid_spec=pltpu.PrefetchScalarGridSpec(
            num_scalar_prefetch=1, grid=(steps,),
            in_specs=[blocked], out_specs=pl.BlockSpec(memory_space=pltpu.SMEM)),
        out_shape=jax.ShapeDtypeStruct((n_slots,), jnp.int32),
        compiler_params=_cparams(("arbitrary",)),
        name="moe_invert",
    )(meta, sot.reshape(steps, 1, INVERT_CHUNK))


def _experts_kernel(texp_ref, nused_ref, tos_ref, tab_ref, w13_ref, w2_ref, ys_ref,
                    xp_ref, xs_ref, acc_ref):
    j = pl.program_id(0)
    c = pl.program_id(1)
    used = j < nused_ref[0]
    last = c == pl.num_programs(1) - 1

    @pl.when(used & (c == 0))
    def _():
        def gather(s, carry):
            xp_ref[pl.ds(s, 1), :] = tab_ref[pl.ds(tos_ref[0, s], 1), :]
            return carry

        lax.fori_loop(0, MOE_TILE, gather, 0, unroll=8)
        xs_ref[...] = _unpack_halves(xp_ref[...]).astype(xs_ref.dtype)
        acc_ref[...] = jnp.zeros(acc_ref.shape, jnp.float32)

    @pl.when(used)
    def _():
        ag = jnp.dot(xs_ref[...], w13_ref[...], preferred_element_type=jnp.float32)
        fc = ag.shape[1] // 2
        a = ag[:, :fc]
        act = (a * jax.nn.sigmoid(a) * ag[:, fc:]).astype(jnp.bfloat16)
        acc_ref[...] += jnp.dot(act, w2_ref[...].astype(jnp.bfloat16), preferred_element_type=jnp.float32)

    @pl.when(used & last)
    def _():
        ys_ref[...] = _pack_halves(acc_ref[...])

    @pl.when(jnp.logical_not(used) & last)
    def _():
        ys_ref[...] = _pack_halves(jnp.zeros(acc_ref.shape, jnp.float32))


def _experts(tile_expert, n_used, tos, table, w1, w3, w2):
    n_tiles = tos.shape[0]
    n_tok, half = table.shape
    d = 2 * half
    n_exp, _, ff = w1.shape
    fc = MOE_FF_CHUNK
    nc = ff // fc
    assert ff % fc == 0
    w13 = jnp.concatenate([w[:, :, c * fc:(c + 1) * fc] for c in range(nc) for w in (w1, w3)], axis=-1)
    chunk = lambda j, c, te, nu: jnp.where(j < nu[0], c, nc - 1)
    return pl.pallas_call(
        _experts_kernel,
        grid_spec=pltpu.PrefetchScalarGridSpec(
            num_scalar_prefetch=2, grid=(n_tiles, nc),
            in_specs=[pl.BlockSpec((None, 1, MOE_TILE), lambda j, c, te, nu: (j, 0, 0),
                                   memory_space=pltpu.SMEM),
                      pl.BlockSpec((n_tok, half), lambda j, c, te, nu: (0, 0),
                                   pipeline_mode=pl.Buffered(1)),
                      pl.BlockSpec((None, d, 2 * fc), lambda j, c, te, nu: (te[j], 0, chunk(j, c, te, nu))),
                      pl.BlockSpec((None, fc, d), lambda j, c, te, nu: (te[j], chunk(j, c, te, nu), 0))],
            out_specs=pl.BlockSpec((MOE_TILE, half), lambda j, c, te, nu: (j, 0)),
            scratch_shapes=[pltpu.VMEM((MOE_TILE, half), jnp.uint32),
                            pltpu.VMEM((MOE_TILE, d), jnp.bfloat16),
                            pltpu.VMEM((MOE_TILE, d), jnp.float32)]),
        out_shape=jax.ShapeDtypeStruct((n_tiles * MOE_TILE, half), jnp.uint32),
        compiler_params=_cparams(("arbitrary", "arbitrary")),
        name="moe_experts",
    )(tile_expert, n_used, tos, table, w13, w2)


def _combine_kernel(win_ref, x1_ref, route_ref, wrow_ref, *rest, n_exp):
    ys_refs = rest[:2 * n_exp]
    mod_ref, gain_ref, o_ref, w_ref, r_ref = rest[2 * n_exp:]
    tm = x1_ref.shape[0]
    for blk, ys_ref in enumerate(ys_refs):
        w_ref[pl.ds(blk * tm, tm), :] = ys_ref[...]

    def fetch(r, carry):
        for k in range(TOP_K):
            r_ref[k, pl.ds(r, 1), :] = w_ref[pl.ds(wrow_ref[0, TOP_K * r + k], 1), :]
        return carry

    lax.fori_loop(0, tm, fetch, 0, unroll=8)
    route = route_ref[...]
    y = route[:, 2:3] * _unpack_halves(r_ref[0]) + route[:, 3:4] * _unpack_halves(r_ref[1])
    x2 = x1_ref[...] + mod_ref[5:6, :] * y
    o_ref[...] = _rms(x2) * gain_ref[...]


def _combine_final(win, x1, route, wrow3, ys, modtab, final_gain, tiles_per_batch, n_exp):
    n_tok, d = x1.shape
    tm = TOKEN_TILE
    n_assign = tm * TOP_K
    ys_specs = [pl.BlockSpec((tm, d // 2), lambda i, win, e=e, jj=jj: (win[i * n_exp + e] + jj, 0))
                for e in range(n_exp) for jj in range(2)]
    smem_blk = pl.BlockSpec((None, 1, n_assign), lambda i, win: (i, 0, 0), memory_space=pltpu.SMEM)
    return pl.pallas_call(
        functools.partial(_combine_kernel, n_exp=n_exp),
        grid_spec=pltpu.PrefetchScalarGridSpec(
            num_scalar_prefetch=1, grid=(n_tok // tm,),
            in_specs=[pl.BlockSpec((tm, d), lambda i, win: (i, 0)),
                      pl.BlockSpec((tm, LANES), lambda i, win: (i, 0)),
                      smem_blk, *ys_specs,
                      pl.BlockSpec((None, 6, d), lambda i, win: ((i // tiles_per_batch) * 2, 0, 0)),
                      pl.BlockSpec((1, d), lambda i, win: (0, 0))],
            out_specs=pl.BlockSpec((tm, d), lambda i, win: (i, 0)),
            scratch_shapes=[pltpu.VMEM((2 * n_exp * tm, d // 2), jnp.uint32),
                            pltpu.VMEM((TOP_K, tm, d // 2), jnp.uint32)]),
        out_shape=jax.ShapeDtypeStruct((n_tok, d), jnp.float32),
        compiler_params=_cparams(("arbitrary",)),
        name="moe_combine",
    )(win, x1, route, wrow3, *([ys] * (2 * n_exp)), modtab, final_gain.reshape(1, d))


def _moe_ffn_final(x1, h2p, route, w1, w3, w2, modtab, final_gain):
    b, n, d = x1.shape
    n_tok = b * n
    n_exp = w1.shape[0]
    n_tiles = n_tok * TOP_K // MOE_TILE + n_exp + 1
    eid = route[..., :TOP_K].astype(jnp.int32).reshape(n_tok * TOP_K)
    meta, tile_expert = _route_plan(eid, n_exp, n_tiles)
    sot3, wrow3, win = _slots(eid, meta[:n_exp], n_exp)
    tos = _invert(meta, sot3, n_exp, n_tiles * MOE_TILE)
    ys = _experts(tile_expert, meta[3 * n_exp:], tos.reshape(n_tiles, 1, MOE_TILE),
                  h2p, w1, w3, w2)
    out = _combine_final(win[:, :, 0].reshape(-1), x1.reshape(n_tok, d), route.reshape(n_tok, LANES),
                         wrow3, ys, modtab, final_gain, n // TOKEN_TILE, n_exp)
    return out.reshape(b, n, d)


def _rope_tables(n_lat, n_ctx):
    t = jnp.arange(n_lat, dtype=jnp.int32)
    n_freq = HEAD_DIM // 4
    inv_freq = ROPE_BASE ** (-jnp.arange(n_freq, dtype=jnp.float32) / n_freq)
    ang = jnp.concatenate([(t // GRID_W).astype(jnp.float32)[:, None] * inv_freq,
                           (t % GRID_W).astype(jnp.float32)[:, None] * inv_freq], axis=-1)
    cos, sin = jnp.cos(ang), jnp.sin(ang)
    reps = LANES // HEAD_DIM
    cos_t = jnp.tile(jnp.concatenate([cos, cos], axis=-1), (1, reps))
    sin_t = jnp.tile(jnp.concatenate([-sin, sin], axis=-1), (1, reps))
    cos_t = jnp.concatenate([cos_t, jnp.ones((n_ctx, LANES), jnp.float32)], axis=0)
    sin_t = jnp.concatenate([sin_t, jnp.zeros((n_ctx, LANES), jnp.float32)], axis=0)
    return cos_t, sin_t


def kernel(x, c, ctx, c_ctx, ada_w, ada_b, w_in, w_out, na_rpb, diff_lambda, diff_subln, conv_w,
           ffn_w1, ffn_w3, ffn_w2, router_w, moe_w1, moe_w3, moe_w2, final_gain):
    b, n, d = x.shape
    n_ctx = ctx.shape[1]
    depth = w_in.shape[0]
    assert d == D_MODEL and n % TOKEN_TILE == 0 and n_ctx == TOKEN_TILE and b + 1 <= 8
    assert depth == 2, "layer 0 dense with a context stream, layer 1 routed and final"
    bf = jnp.bfloat16

    stream = (x, ctx, 0)
    cvec = jnp.zeros((8, d), jnp.float32).at[:b].set(c).at[b].set(c_ctx)
    mod = _modulation(cvec, ada_w, ada_b).reshape(depth, 8, 6, d)
    cos_t, sin_t = _rope_tables(n, n_ctx)
    rows = n // GRID_W

    out = None
    for i in range(depth):
        lambda_init = 0.8 - 0.6 * math.exp(-0.3 * i)
        ctx_out = i < depth - 1
        modtab = jnp.stack([mod[i, :b], jnp.broadcast_to(mod[i, b], (b, 6, d))], axis=1).reshape(2 * b, 6, d)
        q, k, v, u, bg = _in_proj(*stream, modtab, cos_t, sin_t, w_in[i].astype(bf), n)
        bias = _na_bias_table(na_rpb[i], rows)
        na = _na_latent(q, k, v, bias, n)
        gain = diff_subln[i].reshape(1, 2 * HEAD_DIM)
        dif = _diff_attention(q, k, v, diff_lambda[i], gain, lambda_init,
                              q_rows=n, q_start=0, tq=1024, k_rows=n + n_ctx, k_start=0, tk=1408,
                              heads_per_step=4)
        if ctx_out:
            na_c = _na_context(q, k, v, n)
            dif_c = _diff_attention(q, k, v, diff_lambda[i], gain, lambda_init,
                                    q_rows=n_ctx, q_start=n, tq=n_ctx, k_rows=n_ctx, k_start=n, tk=n_ctx,
                                    heads_per_step=4)
            na = jnp.concatenate([na, na_c], axis=1)
            dif = jnp.concatenate([dif, dif_c], axis=1)
            m = i // 2
            x_all = _out_proj(*stream, na, dif, u, bg, conv_w[i], w_out[i].astype(bf), modtab, n, n + n_ctx,
                              ffn=(ffn_w1[m].astype(bf), ffn_w3[m].astype(bf), ffn_w2[m].astype(bf)))
            stream = (x_all, x_all, n // TOKEN_TILE)
        else:
            m = i // 2
            rw = jnp.zeros((d, LANES), jnp.float32).at[:, :N_EXPERTS].set(router_w[m])
            rw_hi = rw.astype(bf)
            rw = jnp.stack([rw_hi, (rw - rw_hi.astype(jnp.float32)).astype(bf)])
            x1, h2p, route = _out_proj(*stream, na, dif, u, bg, conv_w[i], w_out[i].astype(bf), modtab, n, n,
                                       router_w=rw)
            out = _moe_ffn_final(x1, h2p, route, moe_w1[m].astype(bf), moe_w3[m].astype(bf),
                                 moe_w2[m], modtab, final_gain)
    return out
```

```python
import functools
import math

import numpy as np
import jax
import jax.numpy as jnp
from jax import lax
from jax.experimental import pallas as pl
from jax.experimental.pallas import tpu as pltpu

D_MODEL = 1024
GRID_W = 64
HEAD_DIM = 64
NA_HEADS = 4
NA_WIDTH = NA_HEADS * HEAD_DIM
WIN_H = 8
WIN_W = 16
DIFF_HEADS = 4
DIFF_WIDTH = DIFF_HEADS * 2 * HEAD_DIM
CONV_WIDTH = 256
QKV_COLS = NA_WIDTH + DIFF_WIDTH
NA_COL_BLOCK = DIFF_WIDTH // NA_WIDTH
IN_COLS = 3 * QKV_COLS + 3 * CONV_WIDTH
ROPE_BASE = 10000.0
N_EXPERTS = 8
EPS = 1e-6
NEG_BIG = -1e30
LOG2E = math.log2(math.e)

LANES = 128
TOKEN_TILE = 256
NA_ROWS = 4
NA_KEY_ROWS = NA_ROWS + WIN_H
VMEM_LIMIT = 56 * 1024 * 1024


def _cparams(sem):
    return pltpu.CompilerParams(dimension_semantics=sem, vmem_limit_bytes=VMEM_LIMIT)


def _rms(x):
    return x * lax.rsqrt(jnp.mean(x * x, axis=-1, keepdims=True) + EPS)


def _pack_halves(x):
    half = x.shape[1] // 2
    return pltpu.pack_elementwise([x[:, :half], x[:, half:]], packed_dtype=jnp.bfloat16)


def _unpack_halves(p):
    parts = [pltpu.unpack_elementwise(p, index=i, packed_dtype=jnp.bfloat16, unpacked_dtype=jnp.float32)
             for i in range(2)]
    return jnp.concatenate(parts, axis=1)


def _mod_kernel(c_ref, w_ref, b_ref, o_ref):
    cv = c_ref[...]
    s = cv * jax.nn.sigmoid(cv)
    o_ref[...] = jnp.dot(s, w_ref[...], preferred_element_type=jnp.float32,
                         precision=lax.Precision.HIGHEST) + b_ref[...]


def _modulation(cvec, ada_w, ada_b):
    depth, d, cols = ada_w.shape
    cb = 1536
    return pl.pallas_call(
        _mod_kernel,
        grid=(depth, cols // cb),
        in_specs=[pl.BlockSpec((8, d), lambda i, j: (0, 0)),
                  pl.BlockSpec((None, d, cb), lambda i, j: (i, 0, j)),
                  pl.BlockSpec((None, 1, cb), lambda i, j: (i, 0, j))],
        out_specs=pl.BlockSpec((None, 8, cb), lambda i, j: (i, 0, j)),
        out_shape=jax.ShapeDtypeStruct((depth, 8, cols), jnp.float32),
        compiler_params=_cparams(("parallel", "parallel")),
        name="modulation",
    )(cvec, ada_w, ada_b.reshape(depth, 1, cols))


def _rope(z, cos, sin_signed):
    width = z.shape[1]
    reps = width // LANES
    c = jnp.concatenate([cos] * reps, axis=1)
    s = jnp.concatenate([sin_signed] * reps, axis=1)
    lane = lax.broadcasted_iota(jnp.int32, (1, width), 1)
    first_half = (lane % HEAD_DIM) < (HEAD_DIM // 2)
    swapped = jnp.where(first_half,
                        pltpu.roll(z, width - HEAD_DIM // 2, axis=1),
                        pltpu.roll(z, HEAD_DIM // 2, axis=1))
    return z * c + swapped * s


def _stream_specs(lat, n_lat, ctx_block):
    tm = TOKEN_TILE
    nlt = n_lat // tm
    d = lat.shape[-1]
    return [pl.BlockSpec((None, tm, d), lambda bi, i: (bi, jnp.minimum(i, nlt - 1), 0)),
            pl.BlockSpec((None, tm, d), lambda bi, i: (bi, ctx_block, 0))]


def _in_proj_kernel(x_ref, c_ref, mod_ref, cos_ref, sin_ref, w_ref, q_ref, k_ref, v_ref, u_ref, bg_ref,
                    *, n_lat_tiles):
    x = jnp.where(pl.program_id(1) >= n_lat_tiles, c_ref[...], x_ref[...])
    h = _rms(x) * (1.0 + mod_ref[1:2, :]) + mod_ref[0:1, :]
    p = jnp.dot(h.astype(jnp.bfloat16), w_ref[...], preferred_element_type=jnp.float32)
    cos = cos_ref[...]
    sin = sin_ref[...]
    scale = HEAD_DIM ** -0.5 * LOG2E
    q_ref[:, :DIFF_WIDTH] = (_rope(p[:, NA_WIDTH:QKV_COLS], cos, sin) * scale).astype(q_ref.dtype)
    q_ref[:, DIFF_WIDTH:] = (p[:, :NA_WIDTH] * scale).astype(q_ref.dtype)
    k_ref[:, :DIFF_WIDTH] = _rope(p[:, QKV_COLS + NA_WIDTH:2 * QKV_COLS], cos, sin).astype(k_ref.dtype)
    k_ref[:, DIFF_WIDTH:] = p[:, QKV_COLS:QKV_COLS + NA_WIDTH].astype(k_ref.dtype)
    o = 2 * QKV_COLS
    v_ref[:, :DIFF_WIDTH] = p[:, o + NA_WIDTH:o + QKV_COLS].astype(v_ref.dtype)
    v_ref[:, DIFF_WIDTH:] = p[:, o:o + NA_WIDTH].astype(v_ref.dtype)
    o = 3 * QKV_COLS
    xin = p[:, o:o + CONV_WIDTH]
    bg_ref[...] = p[:, o + CONV_WIDTH:o + 2 * CONV_WIDTH]
    u_ref[...] = p[:, o + 2 * CONV_WIDTH:o + 3 * CONV_WIDTH] * xin


def _in_proj(lat, ctx, ctx_block, modtab, cos_t, sin_t, w_bf16, n_lat):
    b, _, d = lat.shape
    tm = TOKEN_TILE
    nlt = n_lat // tm
    t = n_lat + tm
    tok = lambda width: pl.BlockSpec((None, tm, width), lambda bi, i: (bi, i, 0))
    return pl.pallas_call(
        functools.partial(_in_proj_kernel, n_lat_tiles=nlt),
        grid=(b, t // tm),
        in_specs=[*_stream_specs(lat, n_lat, ctx_block),
                  pl.BlockSpec((None, 6, d), lambda bi, i: (bi * 2 + (i >= nlt).astype(jnp.int32), 0, 0)),
                  pl.BlockSpec((tm, LANES), lambda bi, i: (i, 0)),
                  pl.BlockSpec((tm, LANES), lambda bi, i: (i, 0)),
                  pl.BlockSpec((d, IN_COLS), lambda bi, i: (0, 0))],
        out_specs=[tok(QKV_COLS), tok(QKV_COLS), tok(QKV_COLS), tok(CONV_WIDTH), tok(CONV_WIDTH)],
        out_shape=[jax.ShapeDtypeStruct((b, t, QKV_COLS), jnp.bfloat16)] * 3
        + [jax.ShapeDtypeStruct((b, t, CONV_WIDTH), jnp.float32)] * 2,
        compiler_params=_cparams(("parallel", "parallel")),
        name="in_proj",
    )(lat, ctx, modtab, cos_t, sin_t, w_bf16)


def _na_kernel(*refs, n_lat_blocks):
    q_ref = refs[0]
    k_refs = refs[1:2 + n_lat_blocks]
    v_refs = refs[2 + n_lat_blocks:3 + 2 * n_lat_blocks]
    if n_lat_blocks:
        bias_ref, o_ref = refs[3 + 2 * n_lat_blocks:]
    else:
        (o_ref,) = refs[3 + 2 * n_lat_blocks:]
    lane = lax.broadcasted_iota(jnp.int32, (1, LANES), 1)
    lat_keys = n_lat_blocks * TOKEN_TILE
    for pair in range(NA_HEADS // 2):
        cols = slice(pair * LANES, (pair + 1) * LANES)
        q = q_ref[:, cols]
        outs = []
        for hh in range(2):
            head = 2 * pair + hh
            in_head = (lane >= hh * HEAD_DIM) & (lane < (hh + 1) * HEAD_DIM)
            qm = jnp.where(in_head, q, jnp.zeros_like(q))
            s = jnp.concatenate(
                [lax.dot_general(qm, kr[:, cols], (((1,), (1,)), ((), ())),
                                 preferred_element_type=jnp.float32) for kr in k_refs], axis=1)
            if n_lat_blocks:
                s = jnp.concatenate([s[:, :lat_keys] + bias_ref[head], s[:, lat_keys:]], axis=1)
            m = jnp.max(s, axis=-1, keepdims=True)
            eb = jnp.exp2(s - m).astype(jnp.bfloat16)
            acc = None
            for j, vr in enumerate(v_refs):
                v = vr[:, cols]
                part = jnp.dot(eb[:, j * TOKEN_TILE:(j + 1) * TOKEN_TILE],
                               jnp.concatenate([v, jnp.ones_like(v)], axis=1),
                               preferred_element_type=jnp.float32)
                acc = part if acc is None else acc + part
            outs.append(acc[:, :LANES] / acc[:, LANES:])
        o_ref[:, cols] = jnp.where(lane < HEAD_DIM, outs[0], outs[1]).astype(o_ref.dtype)


def _na_bias_table(rpb, rows):
    n_groups = rows // NA_ROWS
    heads = rpb.shape[0]
    padded = jnp.pad(rpb.astype(jnp.float32) * LOG2E, ((0, 0), (0, 0), (GRID_W, GRID_W)))
    toeplitz = jnp.stack([padded[:, :, GRID_W + WIN_W - 1 - qc:2 * GRID_W + WIN_W - 1 - qc]
                          for qc in range(GRID_W)], axis=2)
    qc = np.arange(GRID_W)[:, None]
    kc = np.arange(GRID_W)[None, :]
    col_start = np.clip(qc - WIN_W // 2, 0, GRID_W - WIN_W)
    col_valid = (kc >= col_start) & (kc < col_start + WIN_W)
    toeplitz = jnp.where(col_valid[None, None], toeplitz, NEG_BIG)
    masked = jnp.full((heads, GRID_W, GRID_W), NEG_BIG, jnp.float32)
    tables = []
    for g in (0, 1, n_groups - 1):
        ws = min(max(g - 1, 0), n_groups - 3) * NA_ROWS
        q_rows = []
        for qr in range(g * NA_ROWS, (g + 1) * NA_ROWS):
            row_start = min(max(qr - WIN_H // 2, 0), rows - WIN_H)
            blocks = [toeplitz[:, kr - qr + WIN_H - 1] if row_start <= kr < row_start + WIN_H else masked
                      for kr in range(ws, ws + NA_KEY_ROWS)]
            q_rows.append(jnp.concatenate(blocks, axis=-1))
        tables.append(jnp.concatenate(q_rows, axis=1))
    return jnp.stack(tables)


def _na_latent(q, k, v, bias, n_lat):
    b = q.shape[0]
    tm = TOKEN_TILE
    assert NA_ROWS * GRID_W == tm and NA_KEY_ROWS * GRID_W == 3 * tm
    ng = n_lat // tm
    ctx_blk = n_lat // tm

    def kv_spec(j):
        return pl.BlockSpec((None, tm, NA_WIDTH),
                            lambda bi, g: (bi, jnp.clip(g - 1, 0, ng - 3) + j, NA_COL_BLOCK))

    ctx_spec = pl.BlockSpec((None, tm, NA_WIDTH), lambda bi, g: (bi, ctx_blk, NA_COL_BLOCK))
    variant = lambda bi, g: ((g > 0).astype(jnp.int32) + (g == ng - 1).astype(jnp.int32), 0, 0, 0)
    return pl.pallas_call(
        functools.partial(_na_kernel, n_lat_blocks=3),
        grid=(b, ng),
        in_specs=[pl.BlockSpec((None, tm, NA_WIDTH), lambda bi, g: (bi, g, NA_COL_BLOCK)),
                  kv_spec(0), kv_spec(1), kv_spec(2), ctx_spec,
                  kv_spec(0), kv_spec(1), kv_spec(2), ctx_spec,
                  pl.BlockSpec((None, NA_HEADS, tm, 3 * tm), variant)],
        out_specs=pl.BlockSpec((None, tm, NA_WIDTH), lambda bi, g: (bi, g, 0)),
        out_shape=jax.ShapeDtypeStruct((b, n_lat, NA_WIDTH), jnp.bfloat16),
        compiler_params=_cparams(("parallel", "parallel")),
        name="na_latent",
    )(q, k, k, k, k, v, v, v, v, bias)


def _na_context(q, k, v, n_lat):
    b = q.shape[0]
    tm = TOKEN_TILE
    ctx_spec = pl.BlockSpec((None, tm, NA_WIDTH), lambda bi: (bi, n_lat // tm, NA_COL_BLOCK))
    return pl.pallas_call(
        functools.partial(_na_kernel, n_lat_blocks=0),
        grid=(b,),
        in_specs=[ctx_spec, ctx_spec, ctx_spec],
        out_specs=pl.BlockSpec((None, tm, NA_WIDTH), lambda bi: (bi, 0, 0)),
        out_shape=jax.ShapeDtypeStruct((b, tm, NA_WIDTH), jnp.bfloat16),
        compiler_params=_cparams(("parallel",)),
        name="na_context",
    )(q, k, v)


def _diff_kernel(q_ref, k_ref, v_ref, lam_ref, gain_ref, o_ref, m_ref, acc_ref, *, lambda_init):
    kk = pl.program_id(3)

    @pl.when(kk == 0)
    def _():
        m_ref[...] = jnp.full(m_ref.shape, NEG_BIG, jnp.float32)
        acc_ref[...] = jnp.zeros(acc_ref.shape, jnp.float32)

    lane = lax.broadcasted_iota(jnp.int32, (1, LANES), 1)
    reps = k_ref.shape[0] // LANES
    n_heads = q_ref.shape[1] // LANES
    for vh in range(n_heads):
        cols = slice(vh * LANES, (vh + 1) * LANES)
        q = q_ref[:, cols]
        k = k_ref[:, cols]
        v = v_ref[:, cols]
        v_ext = jnp.concatenate([v, jnp.ones_like(v)], axis=1)
        for hh in range(2):
            idx = 2 * vh + hh
            in_head = (lane >= hh * HEAD_DIM) & (lane < (hh + 1) * HEAD_DIM)
            qm = jnp.where(in_head, q, jnp.zeros_like(q))
            s = lax.dot_general(qm, k, (((1,), (1,)), ((), ())), preferred_element_type=jnp.float32)
            m_old = m_ref[idx]
            m_new = jnp.maximum(m_old, jnp.max(s, axis=-1, keepdims=True))
            alpha = jnp.exp2(m_old - m_new)
            p = jnp.exp2(s - jnp.concatenate([m_new] * reps, axis=1)).astype(jnp.bfloat16)
            pv = jnp.dot(p, v_ext, preferred_element_type=jnp.float32)
            acc_ref[idx] = jnp.concatenate([alpha, alpha], axis=1) * acc_ref[idx] + pv
            m_ref[idx] = m_new

    @pl.when(kk == pl.num_programs(3) - 1)
    def _():
        lp = lam_ref[...]
        lam = (jnp.exp(jnp.sum(lp[0:1] * lp[1:2], axis=-1, keepdims=True))
               - jnp.exp(jnp.sum(lp[2:3] * lp[3:4], axis=-1, keepdims=True)) + lambda_init)
        for vh in range(n_heads):
            a0 = acc_ref[2 * vh]
            a1 = acc_ref[2 * vh + 1]
            o = a0[:, :LANES] / a0[:, LANES:] - lam * (a1[:, :LANES] / a1[:, LANES:])
            o_ref[:, vh * LANES:(vh + 1) * LANES] = (
                _rms(o) * gain_ref[...] * (1.0 - lambda_init)).astype(o_ref.dtype)


def _diff_attention(q, k, v, lam_params, gain, lambda_init, *, q_rows, q_start, tq, k_rows, k_start, tk,
                    heads_per_step):
    b = q.shape[0]
    nq, nk = q_rows // tq, k_rows // tk
    qo, ko = q_start // tq, k_start // tk
    width = heads_per_step * LANES
    return pl.pallas_call(
        functools.partial(_diff_kernel, lambda_init=lambda_init),
        grid=(b, DIFF_HEADS // heads_per_step, nq, nk),
        in_specs=[pl.BlockSpec((None, tq, width), lambda bi, h, i, j: (bi, qo + i, h)),
                  pl.BlockSpec((None, tk, width), lambda bi, h, i, j: (bi, ko + j, h)),
                  pl.BlockSpec((None, tk, width), lambda bi, h, i, j: (bi, ko + j, h)),
                  pl.BlockSpec((4, HEAD_DIM), lambda bi, h, i, j: (0, 0)),
                  pl.BlockSpec((1, LANES), lambda bi, h, i, j: (0, 0))],
        out_specs=pl.BlockSpec((None, tq, width), lambda bi, h, i, j: (bi, i, h)),
        out_shape=jax.ShapeDtypeStruct((b, q_rows, DIFF_WIDTH), jnp.bfloat16),
        scratch_shapes=[pltpu.VMEM((2 * heads_per_step, tq, LANES), jnp.float32),
                        pltpu.VMEM((2 * heads_per_step, tq, 2 * LANES), jnp.float32)],
        compiler_params=_cparams(("parallel", "parallel", "parallel", "arbitrary")),
        name="diff_attention",
    )(q, k, v, lam_params, gain)


def _out_proj_kernel(x_ref, c_ref, na_ref, dif_ref, u_ref, up_ref, un_ref, bg_ref, cw_ref, w_ref, mod_ref,
                     *rest, n_lat_tiles, router):
    if router:
        rw_ref, x1_ref, h2_ref, route_ref = rest
    else:
        w1_ref, w3_ref, w2_ref, o_ref = rest
    i = pl.program_id(1)
    tm = x_ref.shape[0]
    first_of_seq = (i == 0) | (i == n_lat_tiles)
    last_of_seq = (i == n_lat_tiles - 1) | (i == n_lat_tiles)
    u = u_ref[...]
    up = jnp.where(first_of_seq, 0.0, up_ref[7:8, :])
    un = jnp.where(last_of_seq, 0.0, un_ref[0:1, :])
    row = lax.broadcasted_iota(jnp.int32, (tm, 1), 0)
    u_prev = jnp.where(row == 0, up, pltpu.roll(u, 1, axis=0))
    u_next = jnp.where(row == tm - 1, un, pltpu.roll(u, tm - 1, axis=0))
    conv = bg_ref[...] * (cw_ref[0:1, :] * u_prev + cw_ref[1:2, :] * u + cw_ref[2:3, :] * u_next)
    o1 = NA_WIDTH
    o2 = NA_WIDTH + DIFF_WIDTH
    mix = (jnp.dot(na_ref[...], w_ref[:o1, :], preferred_element_type=jnp.float32)
           + jnp.dot(dif_ref[...], w_ref[o1:o2, :], preferred_element_type=jnp.float32)
           + jnp.dot(conv.astype(jnp.bfloat16), w_ref[o2:, :], preferred_element_type=jnp.float32))
    x1 = jnp.where(i >= n_lat_tiles, c_ref[...], x_ref[...]) + mod_ref[2:3, :] * mix
    h2 = _rms(x1) * (1.0 + mod_ref[4:5, :]) + mod_ref[3:4, :]
    if not router:
        h = h2.astype(jnp.bfloat16)
        a = jnp.dot(h, w1_ref[...], preferred_element_type=jnp.float32)
        g = jnp.dot(h, w3_ref[...], preferred_element_type=jnp.float32)
        act = (a * jax.nn.sigmoid(a) * g).astype(jnp.bfloat16)
        y = jnp.dot(act, w2_ref[...], preferred_element_type=jnp.float32)
        o_ref[...] = x1 + mod_ref[5:6, :] * y
    else:
        x1_ref[...] = x1
        h2_ref[...] = _pack_halves(h2)
        h_hi = h2.astype(jnp.bfloat16)
        h_lo = (h2 - h_hi.astype(jnp.float32)).astype(jnp.bfloat16)
        logits = (jnp.dot(h_hi, rw_ref[0], preferred_element_type=jnp.float32)
                  + (jnp.dot(h_hi, rw_ref[1], preferred_element_type=jnp.float32)
                     + jnp.dot(h_lo, rw_ref[0], preferred_element_type=jnp.float32)))
        lane = lax.broadcasted_iota(jnp.int32, logits.shape, 1)
        logits = jnp.where(lane < N_EXPERTS, logits, NEG_BIG)
        m1 = jnp.max(logits, axis=-1, keepdims=True)
        i1 = jnp.min(jnp.where(logits == m1, lane, LANES), axis=-1, keepdims=True)
        rest_l = jnp.where(lane == i1, NEG_BIG, logits)
        m2 = jnp.max(rest_l, axis=-1, keepdims=True)
        i2 = jnp.min(jnp.where(rest_l == m2, lane, LANES), axis=-1, keepdims=True)
        e2 = jnp.exp(m2 - m1)
        w1 = 1.0 / (1.0 + e2)
        route_ref[...] = jnp.where(
            lane == 0, i1.astype(jnp.float32),
            jnp.where(lane == 1, i2.astype(jnp.float32),
                      jnp.where(lane == 2, w1, jnp.where(lane == 3, e2 * w1, 0.0))))


def _out_proj(lat, ctx, ctx_block, na, dif, u, bg, conv_w, w_bf16, modtab, n_lat, n_rows,
              router_w=None, ffn=None):
    b, _, d = lat.shape
    tm = TOKEN_TILE
    nlt = n_lat // tm
    sub = 8
    n_sub = u.shape[1] // sub
    tok = lambda width: pl.BlockSpec((None, tm, width), lambda bi, i: (bi, i, 0))
    in_specs = [*_stream_specs(lat, n_lat, ctx_block), tok(NA_WIDTH), tok(DIFF_WIDTH), tok(CONV_WIDTH),
                pl.BlockSpec((None, sub, CONV_WIDTH),
                             lambda bi, i: (bi, jnp.maximum(i * (tm // sub) - 1, 0), 0)),
                pl.BlockSpec((None, sub, CONV_WIDTH),
                             lambda bi, i: (bi, jnp.minimum((i + 1) * (tm // sub), n_sub - 1), 0)),
                tok(CONV_WIDTH),
                pl.BlockSpec((3, CONV_WIDTH), lambda bi, i: (0, 0)),
                pl.BlockSpec((d, d), lambda bi, i: (0, 0)),
                pl.BlockSpec((None, 6, d), lambda bi, i: (bi * 2 + (i >= nlt).astype(jnp.int32), 0, 0))]
    args = [lat, ctx, na, dif, u, u, u, bg, conv_w, w_bf16, modtab]
    assert (router_w is None) != (ffn is None)
    if router_w is None:
        const = lambda w: pl.BlockSpec(w.shape, lambda bi, i: (0, 0), pipeline_mode=pl.Buffered(1))
        in_specs += [const(w) for w in ffn]
        out_specs = tok(d)
        out_shape = jax.ShapeDtypeStruct((b, n_rows, d), jnp.float32)
        args += list(ffn)
    else:
        in_specs.append(pl.BlockSpec((2, d, LANES), lambda bi, i: (0, 0, 0)))
        steps = n_rows // tm
        out_specs = [tok(d), pl.BlockSpec((tm, d // 2), lambda bi, i: (bi * steps + i, 0)), tok(LANES)]
        out_shape = [jax.ShapeDtypeStruct((b, n_rows, d), jnp.float32),
                     jax.ShapeDtypeStruct((b * n_rows, d // 2), jnp.uint32),
                     jax.ShapeDtypeStruct((b, n_rows, LANES), jnp.float32)]
        args.append(router_w)
    return pl.pallas_call(
        functools.partial(_out_proj_kernel, n_lat_tiles=nlt, router=router_w is not None),
        grid=(b, n_rows // tm),
        in_specs=in_specs, out_specs=out_specs, out_shape=out_shape,
        compiler_params=_cparams(("parallel", "parallel")),
        name="out_proj_ffn" if router_w is None else "out_proj_route",
    )(*args)


MOE_TILE = 512
MOE_FF_CHUNK = 896
TOP_K = 2


def _route_plan(eid, n_exp, n_tiles):
    experts = jnp.arange(n_exp, dtype=jnp.int32)
    counts = jnp.sum((eid[:, None] == experts[None, :]).astype(jnp.int32), axis=0)
    tiles = (counts + MOE_TILE - 1) // MOE_TILE
    tile_end = jnp.cumsum(tiles)
    n_used = tile_end[-1]
    j = jnp.arange(n_tiles, dtype=jnp.int32)
    owner = jnp.sum((j[:, None] >= tile_end[None, :]).astype(jnp.int32), axis=1)
    last_owner = jnp.sum((n_used - 1 >= tile_end).astype(jnp.int32))
    tile_expert = jnp.where(j < n_used, owner, last_owner).astype(jnp.int32)
    meta = jnp.concatenate([(tile_end - tiles) * MOE_TILE, counts, tile_end * MOE_TILE,
                            n_used[None]]).astype(jnp.int32)
    return meta, tile_expert


def _slot_kernel(eid_ref, start_ref, tri_ref, sot_ref, wrow_ref, win_ref, next_ref):
    @pl.when(pl.program_id(0) == 0)
    def _():
        next_ref[...] = start_ref[...]

    e_row = eid_ref[...]
    expert = lax.broadcasted_iota(jnp.int32, (next_ref.shape[0], e_row.shape[1]), 0)
    hit = e_row == expert
    onehot = jnp.where(hit, 1.0, 0.0)
    before = jnp.dot(onehot.astype(jnp.bfloat16), tri_ref[...], preferred_element_type=jnp.float32)
    nxt = next_ref[:, 0:1]
    blk = jnp.floor(nxt * (1.0 / TOKEN_TILE))
    slot = before + nxt
    sot_ref[...] = jnp.sum(jnp.where(hit, slot, 0.0), axis=0, keepdims=True).astype(jnp.int32)
    wrow = expert.astype(jnp.float32) * (2 * TOKEN_TILE) + slot - blk * TOKEN_TILE
    wrow_ref[...] = jnp.sum(jnp.where(hit, wrow, 0.0), axis=0, keepdims=True).astype(jnp.int32)
    win_ref[...] = jnp.broadcast_to(blk, win_ref.shape).astype(jnp.int32)
    next_ref[...] = next_ref[...] + jnp.sum(onehot, axis=1, keepdims=True)


def _slots(eid, start, n_exp):
    n_assign = TOKEN_TILE * TOP_K
    steps = eid.shape[0] // n_assign
    assert n_exp == 8, "experts are laid along the eight sublanes"
    tri = jnp.asarray(np.triu(np.ones((n_assign, n_assign), np.float32), k=1), jnp.bfloat16)
    row = pl.BlockSpec((None, 1, n_assign), lambda i: (i, 0, 0))
    return pl.pallas_call(
        _slot_kernel,
        grid=(steps,),
        in_specs=[row, pl.BlockSpec((n_exp, LANES), lambda i: (0, 0)),
                  pl.BlockSpec((n_assign, n_assign), lambda i: (0, 0))],
        out_specs=[row, row, pl.BlockSpec((None, n_exp, LANES), lambda i: (i, 0, 0))],
        out_shape=[jax.ShapeDtypeStruct((steps, 1, n_assign), jnp.int32),
                   jax.ShapeDtypeStruct((steps, 1, n_assign), jnp.int32),
                   jax.ShapeDtypeStruct((steps, n_exp, LANES), jnp.int32)],
        scratch_shapes=[pltpu.VMEM((n_exp, LANES), jnp.float32)],
        compiler_params=_cparams(("arbitrary",)),
        name="moe_slots",
    )(eid.reshape(steps, 1, n_assign),
      jnp.broadcast_to(start.astype(jnp.float32)[:, None], (n_exp, LANES)), tri)


INVERT_CHUNK = 4096


def _invert_kernel(meta_ref, sot_ref, tos_ref, *, n_exp, n_slots):
    i = pl.program_id(0)
    n_assign = sot_ref.shape[1]

    def clear(s, carry):
        tos_ref[s] = 0
        return carry

    @pl.when(i == 0)
    def _():
        for e in range(n_exp):
            lax.fori_loop(meta_ref[e] + meta_ref[n_exp + e], meta_ref[2 * n_exp + e], clear, 0)
        lax.fori_loop(meta_ref[3 * n_exp] * MOE_TILE, n_slots, clear, 0)

    first = i * (n_assign // TOP_K)

    def put(t, carry):
        for k in range(TOP_K):
            tos_ref[sot_ref[0, TOP_K * t + k]] = first + t
        return carry

    lax.fori_loop(0, n_assign // TOP_K, put, 0, unroll=8)


def _invert(meta, sot, n_exp, n_slots):
    steps = sot.size // INVERT_CHUNK
    blocked = pl.BlockSpec((None, 1, INVERT_CHUNK), lambda i, meta: (i, 0, 0), memory_space=pltpu.SMEM)
    return pl.pallas_call(
        functools.partial(_invert_kernel, n_exp=n_exp, n_slots=n_slots),
        grid_spec=pltpu.PrefetchScalarGridSpec(
            num_scalar_prefetch=1, grid=(steps,),
            in_specs=[blocked], out_specs=pl.BlockSpec(memory_space=pltpu.SMEM)),
        out_shape=jax.ShapeDtypeStruct((n_slots,), jnp.int32),
        compiler_params=_cparams(("arbitrary",)),
        name="moe_invert",
    )(meta, sot.reshape(steps, 1, INVERT_CHUNK))


def _experts_kernel(texp_ref, nused_ref, tos_ref, tab_ref, w13_ref, w2_ref, ys_ref,
                    xp_ref, xs_ref, acc_ref):
    j = pl.program_id(0)
    c = pl.program_id(1)
    used = j < nused_ref[0]
    last = c == pl.num_programs(1) - 1

    @pl.when(used & (c == 0))
    def _():
        def gather(s, carry):
            xp_ref[pl.ds(s, 1), :] = tab_ref[pl.ds(tos_ref[0, s], 1), :]
            return carry

        lax.fori_loop(0, MOE_TILE, gather, 0, unroll=8)
        xs_ref[...] = _unpack_halves(xp_ref[...]).astype(xs_ref.dtype)
        acc_ref[...] = jnp.zeros(acc_ref.shape, jnp.float32)

    @pl.when(used)
    def _():
        ag = jnp.dot(xs_ref[...], w13_ref[...], preferred_element_type=jnp.float32)
        fc = ag.shape[1] // 2
        a = ag[:, :fc]
        act = (a * jax.nn.sigmoid(a) * ag[:, fc:]).astype(jnp.bfloat16)
        acc_ref[...] += jnp.dot(act, w2_ref[...].astype(jnp.bfloat16), preferred_element_type=jnp.float32)

    @pl.when(used & last)
    def _():
        ys_ref[...] = _pack_halves(acc_ref[...])

    @pl.when(jnp.logical_not(used) & last)
    def _():
        ys_ref[...] = _pack_halves(jnp.zeros(acc_ref.shape, jnp.float32))


def _experts(tile_expert, n_used, tos, table, w1, w3, w2):
    n_tiles = tos.shape[0]
    n_tok, half = table.shape
    d = 2 * half
    ff = w1.shape[2]
    fc = MOE_FF_CHUNK
    nc = ff // fc
    assert ff % fc == 0
    w13 = jnp.concatenate([w[:, :, c * fc:(c + 1) * fc] for c in range(nc) for w in (w1, w3)],
                          axis=-1).astype(jnp.bfloat16)
    chunk = lambda j, c, te, nu: jnp.where(j < nu[0], c, nc - 1)
    return pl.pallas_call(
        _experts_kernel,
        grid_spec=pltpu.PrefetchScalarGridSpec(
            num_scalar_prefetch=2, grid=(n_tiles, nc),
            in_specs=[pl.BlockSpec((None, 1, MOE_TILE), lambda j, c, te, nu: (j, 0, 0),
                                   memory_space=pltpu.SMEM),
                      pl.BlockSpec((n_tok, half), lambda j, c, te, nu: (0, 0),
                                   pipeline_mode=pl.Buffered(1)),
                      pl.BlockSpec((None, d, 2 * fc), lambda j, c, te, nu: (te[j], 0, chunk(j, c, te, nu))),
                      pl.BlockSpec((None, fc, d), lambda j, c, te, nu: (te[j], chunk(j, c, te, nu), 0))],
            out_specs=pl.BlockSpec((MOE_TILE, half), lambda j, c, te, nu: (j, 0)),
            scratch_shapes=[pltpu.VMEM((MOE_TILE, half), jnp.uint32),
                            pltpu.VMEM((MOE_TILE, d), jnp.bfloat16),
                            pltpu.VMEM((MOE_TILE, d), jnp.float32)]),
        out_shape=jax.ShapeDtypeStruct((n_tiles * MOE_TILE, half), jnp.uint32),
        compiler_params=_cparams(("arbitrary", "arbitrary")),
        name="moe_experts",
    )(tile_expert, n_used, tos, table, w13, w2)


def _combine_kernel(win_ref, x1_ref, route_ref, wrow_ref, *rest, n_exp):
    ys_refs = rest[:2 * n_exp]
    mod_ref, gain_ref, o_ref, w_ref, r_ref = rest[2 * n_exp:]
    tm = x1_ref.shape[0]
    for blk, ys_ref in enumerate(ys_refs):
        w_ref[pl.ds(blk * tm, tm), :] = ys_ref[...]

    def fetch(r, carry):
        for k in range(TOP_K):
            r_ref[k, pl.ds(r, 1), :] = w_ref[pl.ds(wrow_ref[0, TOP_K * r + k], 1), :]
        return carry

    lax.fori_loop(0, tm, fetch, 0, unroll=8)
    route = route_ref[...]
    y = route[:, 2:3] * _unpack_halves(r_ref[0]) + route[:, 3:4] * _unpack_halves(r_ref[1])
    x2 = x1_ref[...] + mod_ref[5:6, :] * y
    o_ref[...] = _rms(x2) * gain_ref[...]


def _combine_final(win, x1, route, wrow3, ys, modtab, final_gain, tiles_per_batch, n_exp):
    n_tok, d = x1.shape
    tm = TOKEN_TILE
    n_assign = tm * TOP_K
    ys_specs = [pl.BlockSpec((tm, d // 2), lambda i, win, e=e, jj=jj: (win[i * n_exp + e] + jj, 0))
                for e in range(n_exp) for jj in range(2)]
    smem_blk = pl.BlockSpec((None, 1, n_assign), lambda i, win: (i, 0, 0), memory_space=pltpu.SMEM)
    return pl.pallas_call(
        functools.partial(_combine_kernel, n_exp=n_exp),
        grid_spec=pltpu.PrefetchScalarGridSpec(
            num_scalar_prefetch=1, grid=(n_tok // tm,),
            in_specs=[pl.BlockSpec((tm, d), lambda i, win: (i, 0)),
                      pl.BlockSpec((tm, LANES), lambda i, win: (i, 0)),
                      smem_blk, *ys_specs,
                      pl.BlockSpec((None, 6, d), lambda i, win: ((i // tiles_per_batch) * 2, 0, 0)),
                      pl.BlockSpec((1, d), lambda i, win: (0, 0))],
            out_specs=pl.BlockSpec((tm, d), lambda i, win: (i, 0)),
            scratch_shapes=[pltpu.VMEM((2 * n_exp * tm, d // 2), jnp.uint32),
                            pltpu.VMEM((TOP_K, tm, d // 2), jnp.uint32)]),
        out_shape=jax.ShapeDtypeStruct((n_tok, d), jnp.float32),
        compiler_params=_cparams(("arbitrary",)),
        name="moe_combine",
    )(win, x1, route, wrow3, *([ys] * (2 * n_exp)), modtab, final_gain.reshape(1, d))


def _moe_ffn_final(x1, h2p, route, w1, w3, w2, modtab, final_gain):
    b, n, d = x1.shape
    n_tok = b * n
    n_exp = w1.shape[0]
    n_tiles = n_tok * TOP_K // MOE_TILE + n_exp + 1
    eid = route[..., :TOP_K].astype(jnp.int32).reshape(n_tok * TOP_K)
    meta, tile_expert = _route_plan(eid, n_exp, n_tiles)
    sot3, wrow3, win = _slots(eid, meta[:n_exp], n_exp)
    tos = _invert(meta, sot3, n_exp, n_tiles * MOE_TILE)
    ys = _experts(tile_expert, meta[3 * n_exp:], tos.reshape(n_tiles, 1, MOE_TILE),
                  h2p, w1, w3, w2)
    out = _combine_final(win[:, :, 0].reshape(-1), x1.reshape(n_tok, d), route.reshape(n_tok, LANES),
                         wrow3, ys, modtab, final_gain, n // TOKEN_TILE, n_exp)
    return out.reshape(b, n, d)


def _rope_tables(n_lat, n_ctx):
    t = jnp.arange(n_lat, dtype=jnp.int32)
    n_freq = HEAD_DIM // 4
    inv_freq = ROPE_BASE ** (-jnp.arange(n_freq, dtype=jnp.float32) / n_freq)
    ang = jnp.concatenate([(t // GRID_W).astype(jnp.float32)[:, None] * inv_freq,
                           (t % GRID_W).astype(jnp.float32)[:, None] * inv_freq], axis=-1)
    cos, sin = jnp.cos(ang), jnp.sin(ang)
    reps = LANES // HEAD_DIM
    cos_t = jnp.tile(jnp.concatenate([cos, cos], axis=-1), (1, reps))
    sin_t = jnp.tile(jnp.concatenate([-sin, sin], axis=-1), (1, reps))
    cos_t = jnp.concatenate([cos_t, jnp.ones((n_ctx, LANES), jnp.float32)], axis=0)
    sin_t = jnp.concatenate([sin_t, jnp.zeros((n_ctx, LANES), jnp.float32)], axis=0)
    return cos_t, sin_t


def kernel(x, c, ctx, c_ctx, ada_w, ada_b, w_in, w_out, na_rpb, diff_lambda, diff_subln, conv_w,
           ffn_w1, ffn_w3, ffn_w2, router_w, moe_w1, moe_w3, moe_w2, final_gain):
    b, n, d = x.shape
    n_ctx = ctx.shape[1]
    depth = w_in.shape[0]
    assert d == D_MODEL and n % TOKEN_TILE == 0 and n_ctx == TOKEN_TILE and b + 1 <= 8
    assert depth == 2, "layer 0 dense with a context stream, layer 1 routed and final"
    bf = jnp.bfloat16

    stream = (x, ctx, 0)
    cvec = jnp.zeros((8, d), jnp.float32).at[:b].set(c).at[b].set(c_ctx)
    mod = _modulation(cvec, ada_w, ada_b).reshape(depth, 8, 6, d)
    cos_t, sin_t = _rope_tables(n, n_ctx)
    rows = n // GRID_W

    out = None
    for i in range(depth):
        lambda_init = 0.8 - 0.6 * math.exp(-0.3 * i)
        ctx_out = i < depth - 1
        modtab = jnp.stack([mod[i, :b], jnp.broadcast_to(mod[i, b], (b, 6, d))], axis=1).reshape(2 * b, 6, d)
        q, k, v, u, bg = _in_proj(*stream, modtab, cos_t, sin_t, w_in[i].astype(bf), n)
        bias = _na_bias_table(na_rpb[i], rows)
        na = _na_latent(q, k, v, bias, n)
        gain = diff_subln[i].reshape(1, 2 * HEAD_DIM)
        dif = _diff_attention(q, k, v, diff_lambda[i], gain, lambda_init,
                              q_rows=n, q_start=0, tq=1024, k_rows=n + n_ctx, k_start=0, tk=1408,
                              heads_per_step=4)
        if ctx_out:
            na_c = _na_context(q, k, v, n)
            dif_c = _diff_attention(q, k, v, diff_lambda[i], gain, lambda_init,
                                    q_rows=n_ctx, q_start=n, tq=n_ctx, k_rows=n_ctx, k_start=n, tk=n_ctx,
                                    heads_per_step=4)
            na = jnp.concatenate([na, na_c], axis=1)
            dif = jnp.concatenate([dif, dif_c], axis=1)
            m = i // 2
            x_all = _out_proj(*stream, na, dif, u, bg, conv_w[i], w_out[i].astype(bf), modtab, n, n + n_ctx,
                              ffn=(ffn_w1[m].astype(bf), ffn_w3[m].astype(bf), ffn_w2[m].astype(bf)))
            stream = (x_all, x_all, n // TOKEN_TILE)
        else:
            m = i // 2
            rw = jnp.zeros((d, LANES), jnp.float32).at[:, :N_EXPERTS].set(router_w[m])
            rw_hi = rw.astype(bf)
            rw = jnp.stack([rw_hi, (rw - rw_hi.astype(jnp.float32)).astype(bf)])
            x1, h2p, route = _out_proj(*stream, na, dif, u, bg, conv_w[i], w_out[i].astype(bf), modtab, n, n,
                                       router_w=rw)
            out = _moe_ffn_final(x1, h2p, route, moe_w1[m], moe_w3[m], moe_w2[m], modtab, final_gain)
    return out
```

```python
import functools
import math

import numpy as np
import jax
import jax.numpy as jnp
from jax import lax
from jax.experimental import pallas as pl
from jax.experimental.pallas import tpu as pltpu

D_MODEL = 1024
GRID_W = 64
HEAD_DIM = 64
NA_HEADS = 4
NA_WIDTH = NA_HEADS * HEAD_DIM
WIN_H = 8
WIN_W = 16
DIFF_HEADS = 4
DIFF_WIDTH = DIFF_HEADS * 2 * HEAD_DIM
CONV_WIDTH = 256
QKV_COLS = NA_WIDTH + DIFF_WIDTH
NA_COL_BLOCK = DIFF_WIDTH // NA_WIDTH
IN_COLS = 3 * QKV_COLS + 3 * CONV_WIDTH
ROPE_BASE = 10000.0
N_EXPERTS = 8
EPS = 1e-6
NEG_BIG = -1e30
LOG2E = math.log2(math.e)

LANES = 128
TOKEN_TILE = 256
NA_ROWS = 4
NA_KEY_ROWS = NA_ROWS + WIN_H
VMEM_LIMIT = 56 * 1024 * 1024


def _cparams(sem):
    return pltpu.CompilerParams(dimension_semantics=sem, vmem_limit_bytes=VMEM_LIMIT)


def _rms(x):
    return x * lax.rsqrt(jnp.mean(x * x, axis=-1, keepdims=True) + EPS)


def _pack_halves(x):
    half = x.shape[1] // 2
    return pltpu.pack_elementwise([x[:, :half], x[:, half:]], packed_dtype=jnp.bfloat16)


def _unpack_halves(p):
    parts = [pltpu.unpack_elementwise(p, index=i, packed_dtype=jnp.bfloat16, unpacked_dtype=jnp.float32)
             for i in range(2)]
    return jnp.concatenate(parts, axis=1)


def _mod_kernel(c_ref, w_ref, b_ref, o_ref):
    cv = c_ref[...]
    s = cv * jax.nn.sigmoid(cv)
    o_ref[...] = jnp.dot(s, w_ref[...], preferred_element_type=jnp.float32,
                         precision=lax.Precision.HIGHEST) + b_ref[...]


def _modulation(cvec, ada_w, ada_b):
    depth, d, cols = ada_w.shape
    cb = 1536
    return pl.pallas_call(
        _mod_kernel,
        grid=(depth, cols // cb),
        in_specs=[pl.BlockSpec((8, d), lambda i, j: (0, 0)),
                  pl.BlockSpec((None, d, cb), lambda i, j: (i, 0, j)),
                  pl.BlockSpec((None, 1, cb), lambda i, j: (i, 0, j))],
        out_specs=pl.BlockSpec((None, 8, cb), lambda i, j: (i, 0, j)),
        out_shape=jax.ShapeDtypeStruct((depth, 8, cols), jnp.float32),
        compiler_params=_cparams(("parallel", "parallel")),
        name="modulation",
    )(cvec, ada_w, ada_b.reshape(depth, 1, cols))


def _rope(z, cos, sin_signed):
    width = z.shape[1]
    reps = width // LANES
    c = jnp.concatenate([cos] * reps, axis=1)
    s = jnp.concatenate([sin_signed] * reps, axis=1)
    lane = lax.broadcasted_iota(jnp.int32, (1, width), 1)
    first_half = (lane % HEAD_DIM) < (HEAD_DIM // 2)
    swapped = jnp.where(first_half,
                        pltpu.roll(z, width - HEAD_DIM // 2, axis=1),
                        pltpu.roll(z, HEAD_DIM // 2, axis=1))
    return z * c + swapped * s


def _stream_specs(lat, n_lat, ctx_block):
    tm = TOKEN_TILE
    nlt = n_lat // tm
    d = lat.shape[-1]
    return [pl.BlockSpec((None, tm, d), lambda bi, i: (bi, jnp.minimum(i, nlt - 1), 0)),
            pl.BlockSpec((None, tm, d), lambda bi, i: (bi, ctx_block, 0))]


def _in_proj_kernel(x_ref, c_ref, mod_ref, cos_ref, sin_ref, w_ref, q_ref, k_ref, v_ref, u_ref, bg_ref,
                    *, n_lat_tiles):
    x = jnp.where(pl.program_id(1) >= n_lat_tiles, c_ref[...], x_ref[...])
    h = _rms(x) * (1.0 + mod_ref[1:2, :]) + mod_ref[0:1, :]
    p = jnp.dot(h.astype(jnp.bfloat16), w_ref[...], preferred_element_type=jnp.float32)
    cos = cos_ref[...]
    sin = sin_ref[...]
    scale = HEAD_DIM ** -0.5 * LOG2E
    q_ref[:, :DIFF_WIDTH] = (_rope(p[:, NA_WIDTH:QKV_COLS], cos, sin) * scale).astype(q_ref.dtype)
    q_ref[:, DIFF_WIDTH:] = (p[:, :NA_WIDTH] * scale).astype(q_ref.dtype)
    k_ref[:, :DIFF_WIDTH] = _rope(p[:, QKV_COLS + NA_WIDTH:2 * QKV_COLS], cos, sin).astype(k_ref.dtype)
    k_ref[:, DIFF_WIDTH:] = p[:, QKV_COLS:QKV_COLS + NA_WIDTH].astype(k_ref.dtype)
    o = 2 * QKV_COLS
    v_ref[:, :DIFF_WIDTH] = p[:, o + NA_WIDTH:o + QKV_COLS].astype(v_ref.dtype)
    v_ref[:, DIFF_WIDTH:] = p[:, o:o + NA_WIDTH].astype(v_ref.dtype)
    o = 3 * QKV_COLS
    xin = p[:, o:o + CONV_WIDTH]
    bg_ref[...] = p[:, o + CONV_WIDTH:o + 2 * CONV_WIDTH]
    u_ref[...] = p[:, o + 2 * CONV_WIDTH:o + 3 * CONV_WIDTH] * xin


def _in_proj(lat, ctx, ctx_block, modtab, cos_t, sin_t, w_bf16, n_lat):
    b, _, d = lat.shape
    tm = TOKEN_TILE
    nlt = n_lat // tm
    t = n_lat + tm
    tok = lambda width: pl.BlockSpec((None, tm, width), lambda bi, i: (bi, i, 0))
    return pl.pallas_call(
        functools.partial(_in_proj_kernel, n_lat_tiles=nlt),
        grid=(b, t // tm),
        in_specs=[*_stream_specs(lat, n_lat, ctx_block),
                  pl.BlockSpec((None, 6, d), lambda bi, i: (bi * 2 + (i >= nlt).astype(jnp.int32), 0, 0)),
                  pl.BlockSpec((tm, LANES), lambda bi, i: (i, 0)),
                  pl.BlockSpec((tm, LANES), lambda bi, i: (i, 0)),
                  pl.BlockSpec((d, IN_COLS), lambda bi, i: (0, 0))],
        out_specs=[tok(QKV_COLS), tok(QKV_COLS), tok(QKV_COLS), tok(CONV_WIDTH), tok(CONV_WIDTH)],
        out_shape=[jax.ShapeDtypeStruct((b, t, QKV_COLS), jnp.bfloat16)] * 3
        + [jax.ShapeDtypeStruct((b, t, CONV_WIDTH), jnp.float32)] * 2,
        compiler_params=_cparams(("parallel", "parallel")),
        name="in_proj",
    )(lat, ctx, modtab, cos_t, sin_t, w_bf16)


def _na_kernel(*refs, n_lat_blocks):
    q_ref = refs[0]
    k_refs = refs[1:2 + n_lat_blocks]
    v_refs = refs[2 + n_lat_blocks:3 + 2 * n_lat_blocks]
    if n_lat_blocks:
        bias_ref, o_ref = refs[3 + 2 * n_lat_blocks:]
    else:
        (o_ref,) = refs[3 + 2 * n_lat_blocks:]
    lane = lax.broadcasted_iota(jnp.int32, (1, LANES), 1)
    lat_keys = n_lat_blocks * TOKEN_TILE
    for pair in range(NA_HEADS // 2):
        cols = slice(pair * LANES, (pair + 1) * LANES)
        q = q_ref[:, cols]
        outs = []
        for hh in range(2):
            head = 2 * pair + hh
            in_head = (lane >= hh * HEAD_DIM) & (lane < (hh + 1) * HEAD_DIM)
            qm = jnp.where(in_head, q, jnp.zeros_like(q))
            s = jnp.concatenate(
                [lax.dot_general(qm, kr[:, cols], (((1,), (1,)), ((), ())),
                                 preferred_element_type=jnp.float32) for kr in k_refs], axis=1)
            if n_lat_blocks:
                s = jnp.concatenate([s[:, :lat_keys] + bias_ref[head], s[:, lat_keys:]], axis=1)
            m = jnp.max(s, axis=-1, keepdims=True)
            eb = jnp.exp2(s - m).astype(jnp.bfloat16)
            acc = None
            for j, vr in enumerate(v_refs):
                v = vr[:, cols]
                part = jnp.dot(eb[:, j * TOKEN_TILE:(j + 1) * TOKEN_TILE],
                               jnp.concatenate([v, jnp.ones_like(v)], axis=1),
                               preferred_element_type=jnp.float32)
                acc = part if acc is None else acc + part
            outs.append(acc[:, :LANES] / acc[:, LANES:])
        o_ref[:, cols] = jnp.where(lane < HEAD_DIM, outs[0], outs[1]).astype(o_ref.dtype)


def _na_bias_table(rpb, rows):
    n_groups = rows // NA_ROWS
    heads = rpb.shape[0]
    padded = jnp.pad(rpb.astype(jnp.float32) * LOG2E, ((0, 0), (0, 0), (GRID_W, GRID_W)))
    toeplitz = jnp.stack([padded[:, :, GRID_W + WIN_W - 1 - qc:2 * GRID_W + WIN_W - 1 - qc]
                          for qc in range(GRID_W)], axis=2)
    qc = np.arange(GRID_W)[:, None]
    kc = np.arange(GRID_W)[None, :]
    col_start = np.clip(qc - WIN_W // 2, 0, GRID_W - WIN_W)
    col_valid = (kc >= col_start) & (kc < col_start + WIN_W)
    toeplitz = jnp.where(col_valid[None, None], toeplitz, NEG_BIG)
    masked = jnp.full((heads, GRID_W, GRID_W), NEG_BIG, jnp.float32)
    tables = []
    for g in (0, 1, n_groups - 1):
        ws = min(max(g - 1, 0), n_groups - 3) * NA_ROWS
        q_rows = []
        for qr in range(g * NA_ROWS, (g + 1) * NA_ROWS):
            row_start = min(max(qr - WIN_H // 2, 0), rows - WIN_H)
            blocks = [toeplitz[:, kr - qr + WIN_H - 1] if row_start <= kr < row_start + WIN_H else masked
                      for kr in range(ws, ws + NA_KEY_ROWS)]
            q_rows.append(jnp.concatenate(blocks, axis=-1))
        tables.append(jnp.concatenate(q_rows, axis=1))
    return jnp.stack(tables)


def _na_latent(q, k, v, bias, n_lat):
    b = q.shape[0]
    tm = TOKEN_TILE
    assert NA_ROWS * GRID_W == tm and NA_KEY_ROWS * GRID_W == 3 * tm
    ng = n_lat // tm
    ctx_blk = n_lat // tm

    def kv_spec(j):
        return pl.BlockSpec((None, tm, NA_WIDTH),
                            lambda bi, g: (bi, jnp.clip(g - 1, 0, ng - 3) + j, NA_COL_BLOCK))

    ctx_spec = pl.BlockSpec((None, tm, NA_WIDTH), lambda bi, g: (bi, ctx_blk, NA_COL_BLOCK))
    variant = lambda bi, g: ((g > 0).astype(jnp.int32) + (g == ng - 1).astype(jnp.int32), 0, 0, 0)
    return pl.pallas_call(
        functools.partial(_na_kernel, n_lat_blocks=3),
        grid=(b, ng),
        in_specs=[pl.BlockSpec((None, tm, NA_WIDTH), lambda bi, g: (bi, g, NA_COL_BLOCK)),
                  kv_spec(0), kv_spec(1), kv_spec(2), ctx_spec,
                  kv_spec(0), kv_spec(1), kv_spec(2), ctx_spec,
                  pl.BlockSpec((None, NA_HEADS, tm, 3 * tm), variant)],
        out_specs=pl.BlockSpec((None, tm, NA_WIDTH), lambda bi, g: (bi, g, 0)),
        out_shape=jax.ShapeDtypeStruct((b, n_lat, NA_WIDTH), jnp.bfloat16),
        compiler_params=_cparams(("parallel", "parallel")),
        name="na_latent",
    )(q, k, k, k, k, v, v, v, v, bias)


def _na_context(q, k, v, n_lat):
    b = q.shape[0]
    tm = TOKEN_TILE
    ctx_spec = pl.BlockSpec((None, tm, NA_WIDTH), lambda bi: (bi, n_lat // tm, NA_COL_BLOCK))
    return pl.pallas_call(
        functools.partial(_na_kernel, n_lat_blocks=0),
        grid=(b,),
        in_specs=[ctx_spec, ctx_spec, ctx_spec],
        out_specs=pl.BlockSpec((None, tm, NA_WIDTH), lambda bi: (bi, 0, 0)),
        out_shape=jax.ShapeDtypeStruct((b, tm, NA_WIDTH), jnp.bfloat16),
        compiler_params=_cparams(("parallel",)),
        name="na_context",
    )(q, k, v)


def _diff_kernel(q_ref, k_ref, v_ref, lam_ref, gain_ref, o_ref, m_ref, acc_ref, *, lambda_init):
    kk = pl.program_id(3)

    @pl.when(kk == 0)
    def _():
        m_ref[...] = jnp.full(m_ref.shape, NEG_BIG, jnp.float32)
        acc_ref[...] = jnp.zeros(acc_ref.shape, jnp.float32)

    lane = lax.broadcasted_iota(jnp.int32, (1, LANES), 1)
    reps = k_ref.shape[0] // LANES
    n_heads = q_ref.shape[1] // LANES
    for vh in range(n_heads):
        cols = slice(vh * LANES, (vh + 1) * LANES)
        q = q_ref[:, cols]
        k = k_ref[:, cols]
        v = v_ref[:, cols]
        v_ext = jnp.concatenate([v, jnp.ones_like(v)], axis=1)
        for hh in range(2):
            idx = 2 * vh + hh
            in_head = (lane >= hh * HEAD_DIM) & (lane < (hh + 1) * HEAD_DIM)
            qm = jnp.where(in_head, q, jnp.zeros_like(q))
            s = lax.dot_general(qm, k, (((1,), (1,)), ((), ())), preferred_element_type=jnp.float32)
            m_old = m_ref[idx]
            m_new = jnp.maximum(m_old, jnp.max(s, axis=-1, keepdims=True))
            alpha = jnp.exp2(m_old - m_new)
            p = jnp.exp2(s - jnp.concatenate([m_new] * reps, axis=1)).astype(jnp.bfloat16)
            pv = jnp.dot(p, v_ext, preferred_element_type=jnp.float32)
            acc_ref[idx] = jnp.concatenate([alpha, alpha], axis=1) * acc_ref[idx] + pv
            m_ref[idx] = m_new

    @pl.when(kk == pl.num_programs(3) - 1)
    def _():
        lp = lam_ref[...]
        lam = (jnp.exp(jnp.sum(lp[0:1] * lp[1:2], axis=-1, keepdims=True))
               - jnp.exp(jnp.sum(lp[2:3] * lp[3:4], axis=-1, keepdims=True)) + lambda_init)
        for vh in range(n_heads):
            a0 = acc_ref[2 * vh]
            a1 = acc_ref[2 * vh + 1]
            o = a0[:, :LANES] / a0[:, LANES:] - lam * (a1[:, :LANES] / a1[:, LANES:])
            o_ref[:, vh * LANES:(vh + 1) * LANES] = (
                _rms(o) * gain_ref[...] * (1.0 - lambda_init)).astype(o_ref.dtype)


def _diff_attention(q, k, v, lam_params, gain, lambda_init, *, q_rows, q_start, tq, k_rows, k_start, tk,
                    heads_per_step):
    b = q.shape[0]
    nq, nk = q_rows // tq, k_rows // tk
    qo, ko = q_start // tq, k_start // tk
    width = heads_per_step * LANES
    return pl.pallas_call(
        functools.partial(_diff_kernel, lambda_init=lambda_init),
        grid=(b, DIFF_HEADS // heads_per_step, nq, nk),
        in_specs=[pl.BlockSpec((None, tq, width), lambda bi, h, i, j: (bi, qo + i, h)),
                  pl.BlockSpec((None, tk, width), lambda bi, h, i, j: (bi, ko + j, h)),
                  pl.BlockSpec((None, tk, width), lambda bi, h, i, j: (bi, ko + j, h)),
                  pl.BlockSpec((4, HEAD_DIM), lambda bi, h, i, j: (0, 0)),
                  pl.BlockSpec((1, LANES), lambda bi, h, i, j: (0, 0))],
        out_specs=pl.BlockSpec((None, tq, width), lambda bi, h, i, j: (bi, i, h)),
        out_shape=jax.ShapeDtypeStruct((b, q_rows, DIFF_WIDTH), jnp.bfloat16),
        scratch_shapes=[pltpu.VMEM((2 * heads_per_step, tq, LANES), jnp.float32),
                        pltpu.VMEM((2 * heads_per_step, tq, 2 * LANES), jnp.float32)],
        compiler_params=_cparams(("parallel", "parallel", "parallel", "arbitrary")),
        name="diff_attention",
    )(q, k, v, lam_params, gain)


def _out_proj_kernel(x_ref, c_ref, na_ref, dif_ref, u_ref, up_ref, un_ref, bg_ref, cw_ref, w_ref, mod_ref,
                     *rest, n_lat_tiles, router):
    if router:
        rw_ref, x1_ref, h2_ref, route_ref = rest
    else:
        w1_ref, w3_ref, w2_ref, o_ref = rest
    i = pl.program_id(1)
    tm = x_ref.shape[0]
    first_of_seq = (i == 0) | (i == n_lat_tiles)
    last_of_seq = (i == n_lat_tiles - 1) | (i == n_lat_tiles)
    u = u_ref[...]
    up = jnp.where(first_of_seq, 0.0, up_ref[7:8, :])
    un = jnp.where(last_of_seq, 0.0, un_ref[0:1, :])
    row = lax.broadcasted_iota(jnp.int32, (tm, 1), 0)
    u_prev = jnp.where(row == 0, up, pltpu.roll(u, 1, axis=0))
    u_next = jnp.where(row == tm - 1, un, pltpu.roll(u, tm - 1, axis=0))
    conv = bg_ref[...] * (cw_ref[0:1, :] * u_prev + cw_ref[1:2, :] * u + cw_ref[2:3, :] * u_next)
    o1 = NA_WIDTH
    o2 = NA_WIDTH + DIFF_WIDTH
    mix = (jnp.dot(na_ref[...], w_ref[:o1, :], preferred_element_type=jnp.float32)
           + jnp.dot(dif_ref[...], w_ref[o1:o2, :], preferred_element_type=jnp.float32)
           + jnp.dot(conv.astype(jnp.bfloat16), w_ref[o2:, :], preferred_element_type=jnp.float32))
    x1 = jnp.where(i >= n_lat_tiles, c_ref[...], x_ref[...]) + mod_ref[2:3, :] * mix
    h2 = _rms(x1) * (1.0 + mod_ref[4:5, :]) + mod_ref[3:4, :]
    if not router:
        h = h2.astype(jnp.bfloat16)
        a = jnp.dot(h, w1_ref[...], preferred_element_type=jnp.float32)
        g = jnp.dot(h, w3_ref[...], preferred_element_type=jnp.float32)
        act = (a * jax.nn.sigmoid(a) * g).astype(jnp.bfloat16)
        y = jnp.dot(act, w2_ref[...], preferred_element_type=jnp.float32)
        o_ref[...] = x1 + mod_ref[5:6, :] * y
    else:
        x1_ref[...] = x1
        h2_ref[...] = _pack_halves(h2)
        h_hi = h2.astype(jnp.bfloat16)
        h_lo = (h2 - h_hi.astype(jnp.float32)).astype(jnp.bfloat16)
        logits = (jnp.dot(h_hi, rw_ref[0], preferred_element_type=jnp.float32)
                  + (jnp.dot(h_hi, rw_ref[1], preferred_element_type=jnp.float32)
                     + jnp.dot(h_lo, rw_ref[0], preferred_element_type=jnp.float32)))
        lane = lax.broadcasted_iota(jnp.int32, logits.shape, 1)
        logits = jnp.where(lane < N_EXPERTS, logits, NEG_BIG)
        m1 = jnp.max(logits, axis=-1, keepdims=True)
        i1 = jnp.min(jnp.where(logits == m1, lane, LANES), axis=-1, keepdims=True)
        rest_l = jnp.where(lane == i1, NEG_BIG, logits)
        m2 = jnp.max(rest_l, axis=-1, keepdims=True)
        i2 = jnp.min(jnp.where(rest_l == m2, lane, LANES), axis=-1, keepdims=True)
        e2 = jnp.exp(m2 - m1)
        w1 = 1.0 / (1.0 + e2)
        route_ref[...] = jnp.where(
            lane == 0, i1.astype(jnp.float32),
            jnp.where(lane == 1, i2.astype(jnp.float32),
                      jnp.where(lane == 2, w1, jnp.where(lane == 3, e2 * w1, 0.0))))


def _out_proj(lat, ctx, ctx_block, na, dif, u, bg, conv_w, w_bf16, modtab, n_lat, n_rows,
              router_w=None, ffn=None):
    b, _, d = lat.shape
    tm = TOKEN_TILE
    nlt = n_lat // tm
    sub = 8
    n_sub = u.shape[1] // sub
    tok = lambda width: pl.BlockSpec((None, tm, width), lambda bi, i: (bi, i, 0))
    in_specs = [*_stream_specs(lat, n_lat, ctx_block), tok(NA_WIDTH), tok(DIFF_WIDTH), tok(CONV_WIDTH),
                pl.BlockSpec((None, sub, CONV_WIDTH),
                             lambda bi, i: (bi, jnp.maximum(i * (tm // sub) - 1, 0), 0)),
                pl.BlockSpec((None, sub, CONV_WIDTH),
                             lambda bi, i: (bi, jnp.minimum((i + 1) * (tm // sub), n_sub - 1), 0)),
                tok(CONV_WIDTH),
                pl.BlockSpec((3, CONV_WIDTH), lambda bi, i: (0, 0)),
                pl.BlockSpec((d, d), lambda bi, i: (0, 0)),
                pl.BlockSpec((None, 6, d), lambda bi, i: (bi * 2 + (i >= nlt).astype(jnp.int32), 0, 0))]
    args = [lat, ctx, na, dif, u, u, u, bg, conv_w, w_bf16, modtab]
    assert (router_w is None) != (ffn is None)
    if router_w is None:
        const = lambda w: pl.BlockSpec(w.shape, lambda bi, i: (0, 0), pipeline_mode=pl.Buffered(1))
        in_specs += [const(w) for w in ffn]
        out_specs = tok(d)
        out_shape = jax.ShapeDtypeStruct((b, n_rows, d), jnp.float32)
        args += list(ffn)
    else:
        in_specs.append(pl.BlockSpec((2, d, LANES), lambda bi, i: (0, 0, 0)))
        steps = n_rows // tm
        out_specs = [tok(d), pl.BlockSpec((tm, d // 2), lambda bi, i: (bi * steps + i, 0)), tok(LANES)]
        out_shape = [jax.ShapeDtypeStruct((b, n_rows, d), jnp.float32),
                     jax.ShapeDtypeStruct((b * n_rows, d // 2), jnp.uint32),
                     jax.ShapeDtypeStruct((b, n_rows, LANES), jnp.float32)]
        args.append(router_w)
    return pl.pallas_call(
        functools.partial(_out_proj_kernel, n_lat_tiles=nlt, router=router_w is not None),
        grid=(b, n_rows // tm),
        in_specs=in_specs, out_specs=out_specs, out_shape=out_shape,
        compiler_params=_cparams(("parallel", "parallel")),
        name="out_proj_ffn" if router_w is None else "out_proj_route",
    )(*args)


MOE_TILE = 512
MOE_FF_CHUNK = 896
TOP_K = 2


def _route_plan(eid, n_exp, n_tiles):
    experts = jnp.arange(n_exp, dtype=jnp.int32)
    counts = jnp.sum((eid[:, None] == experts[None, :]).astype(jnp.int32), axis=0)
    tiles = (counts + MOE_TILE - 1) // MOE_TILE
    tile_end = jnp.cumsum(tiles)
    n_used = tile_end[-1]
    j = jnp.arange(n_tiles, dtype=jnp.int32)
    owner = jnp.sum((j[:, None] >= tile_end[None, :]).astype(jnp.int32), axis=1)
    last_owner = jnp.sum((n_used - 1 >= tile_end).astype(jnp.int32))
    tile_expert = jnp.where(j < n_used, owner, last_owner).astype(jnp.int32)
    meta = jnp.concatenate([(tile_end - tiles) * MOE_TILE, counts, tile_end * MOE_TILE,
                            n_used[None]]).astype(jnp.int32)
    return meta, tile_expert


def _slot_kernel(eid_ref, start_ref, tri_ref, sot_ref, wrow_ref, win_ref, next_ref):
    @pl.when(pl.program_id(0) == 0)
    def _():
        next_ref[...] = start_ref[...]

    e_row = eid_ref[...]
    expert = lax.broadcasted_iota(jnp.int32, (next_ref.shape[0], e_row.shape[1]), 0)
    hit = e_row == expert
    onehot = jnp.where(hit, 1.0, 0.0)
    before = jnp.dot(onehot.astype(jnp.bfloat16), tri_ref[...], preferred_element_type=jnp.float32)
    nxt = next_ref[:, 0:1]
    blk = jnp.floor(nxt * (1.0 / TOKEN_TILE))
    slot = before + nxt
    sot_ref[...] = jnp.sum(jnp.where(hit, slot, 0.0), axis=0, keepdims=True).astype(jnp.int32)
    wrow = expert.astype(jnp.float32) * (2 * TOKEN_TILE) + slot - blk * TOKEN_TILE
    wrow_ref[...] = jnp.sum(jnp.where(hit, wrow, 0.0), axis=0, keepdims=True).astype(jnp.int32)
    win_ref[...] = jnp.broadcast_to(blk, win_ref.shape).astype(jnp.int32)
    next_ref[...] = next_ref[...] + jnp.sum(onehot, axis=1, keepdims=True)


def _slots(eid, start, n_exp):
    n_assign = TOKEN_TILE * TOP_K
    steps = eid.shape[0] // n_assign
    assert n_exp == 8, "experts are laid along the eight sublanes"
    tri = jnp.asarray(np.triu(np.ones((n_assign, n_assign), np.float32), k=1), jnp.bfloat16)
    row = pl.BlockSpec((None, 1, n_assign), lambda i: (i, 0, 0))
    return pl.pallas_call(
        _slot_kernel,
        grid=(steps,),
        in_specs=[row, pl.BlockSpec((n_exp, LANES), lambda i: (0, 0)),
                  pl.BlockSpec((n_assign, n_assign), lambda i: (0, 0))],
        out_specs=[row, row, pl.BlockSpec((None, n_exp, LANES), lambda i: (i, 0, 0))],
        out_shape=[jax.ShapeDtypeStruct((steps, 1, n_assign), jnp.int32),
                   jax.ShapeDtypeStruct((steps, 1, n_assign), jnp.int32),
                   jax.ShapeDtypeStruct((steps, n_exp, LANES), jnp.int32)],
        scratch_shapes=[pltpu.VMEM((n_exp, LANES), jnp.float32)],
        compiler_params=_cparams(("arbitrary",)),
        name="moe_slots",
    )(eid.reshape(steps, 1, n_assign),
      jnp.broadcast_to(start.astype(jnp.float32)[:, None], (n_exp, LANES)), tri)


INVERT_CHUNK = 4096


def _invert_kernel(meta_ref, sot_ref, tos_ref, *, n_exp, n_slots):
    i = pl.program_id(0)
    n_assign = sot_ref.shape[1]

    def clear(s, carry):
        tos_ref[s] = 0
        return carry

    @pl.when(i == 0)
    def _():
        for e in range(n_exp):
            lax.fori_loop(meta_ref[e] + meta_ref[n_exp + e], meta_ref[2 * n_exp + e], clear, 0)
        lax.fori_loop(meta_ref[3 * n_exp] * MOE_TILE, n_slots, clear, 0)

    first = i * (n_assign // TOP_K)

    def put(t, carry):
        for k in range(TOP_K):
            tos_ref[sot_ref[0, TOP_K * t + k]] = first + t
        return carry

    lax.fori_loop(0, n_assign // TOP_K, put, 0, unroll=8)


def _invert(meta, sot, n_exp, n_slots):
    steps = sot.size // INVERT_CHUNK
    blocked = pl.BlockSpec((None, 1, INVERT_CHUNK), lambda i, meta: (i, 0, 0), memory_space=pltpu.SMEM)
    return pl.pallas_call(
        functools.partial(_invert_kernel, n_exp=n_exp, n_slots=n_slots),
        grid_spec=pltpu.PrefetchScalarGridSpec(
            num_scalar_prefetch=1, grid=(steps,),
            in_specs=[blocked], out_specs=pl.BlockSpec(memory_space=pltpu.SMEM)),
        out_shape=jax.ShapeDtypeStruct((n_slots,), jnp.int32),
        compiler_params=_cparams(("arbitrary",)),
        name="moe_invert",
    )(meta, sot.reshape(steps, 1, INVERT_CHUNK))


def _experts_kernel(texp_ref, nused_ref, tos_ref, tab_ref, w1_ref, w3_ref, w2_ref, ys_ref,
                    xp_ref, xs_ref, acc_ref):
    j = pl.program_id(0)
    c = pl.program_id(1)
    used = j < nused_ref[0]
    last = c == pl.num_programs(1) - 1

    @pl.when(used & (c == 0))
    def _():
        def gather(s, carry):
            xp_ref[pl.ds(s, 1), :] = tab_ref[pl.ds(tos_ref[0, s], 1), :]
            return carry

        lax.fori_loop(0, MOE_TILE, gather, 0, unroll=8)
        xs_ref[...] = _unpack_halves(xp_ref[...]).astype(xs_ref.dtype)
        acc_ref[...] = jnp.zeros(acc_ref.shape, jnp.float32)

    @pl.when(used)
    def _():
        w13 = jnp.concatenate([w1_ref[...], w3_ref[...]], axis=1)
        ag = jnp.dot(xs_ref[...], w13, preferred_element_type=jnp.float32)
        fc = ag.shape[1] // 2
        a = ag[:, :fc]
        act = (a * jax.nn.sigmoid(a) * ag[:, fc:]).astype(jnp.bfloat16)
        acc_ref[...] += jnp.dot(act, w2_ref[...].astype(jnp.bfloat16), preferred_element_type=jnp.float32)

    @pl.when(used & last)
    def _():
        ys_ref[...] = _pack_halves(acc_ref[...])

    @pl.when(jnp.logical_not(used) & last)
    def _():
        ys_ref[...] = _pack_halves(jnp.zeros(acc_ref.shape, jnp.float32))


def _experts(tile_expert, n_used, tos, table, w1, w3, w2):
    n_tiles = tos.shape[0]
    n_tok, half = table.shape
    d = 2 * half
    ff = w1.shape[2]
    fc = MOE_FF_CHUNK
    nc = ff // fc
    assert ff % fc == 0
    chunk = lambda j, c, te, nu: jnp.where(j < nu[0], c, nc - 1)
    return pl.pallas_call(
        _experts_kernel,
        grid_spec=pltpu.PrefetchScalarGridSpec(
            num_scalar_prefetch=2, grid=(n_tiles, nc),
            in_specs=[pl.BlockSpec((None, 1, MOE_TILE), lambda j, c, te, nu: (j, 0, 0),
                                   memory_space=pltpu.SMEM),
                      pl.BlockSpec((n_tok, half), lambda j, c, te, nu: (0, 0),
                                   pipeline_mode=pl.Buffered(1)),
                      pl.BlockSpec((None, d, fc), lambda j, c, te, nu: (te[j], 0, chunk(j, c, te, nu))),
                      pl.BlockSpec((None, d, fc), lambda j, c, te, nu: (te[j], 0, chunk(j, c, te, nu))),
                      pl.BlockSpec((None, fc, d), lambda j, c, te, nu: (te[j], chunk(j, c, te, nu), 0))],
            out_specs=pl.BlockSpec((MOE_TILE, half), lambda j, c, te, nu: (j, 0)),
            scratch_shapes=[pltpu.VMEM((MOE_TILE, half), jnp.uint32),
                            pltpu.VMEM((MOE_TILE, d), jnp.bfloat16),
                            pltpu.VMEM((MOE_TILE, d), jnp.float32)]),
        out_shape=jax.ShapeDtypeStruct((n_tiles * MOE_TILE, half), jnp.uint32),
        compiler_params=_cparams(("arbitrary", "arbitrary")),
        name="moe_experts",
    )(tile_expert, n_used, tos, table, w1, w3, w2)


def _combine_kernel(win_ref, x1_ref, route_ref, wrow_ref, *rest, n_exp):
    ys_refs = rest[:2 * n_exp]
    mod_ref, gain_ref, o_ref, w_ref, r_ref = rest[2 * n_exp:]
    tm = x1_ref.shape[0]
    for blk, ys_ref in enumerate(ys_refs):
        w_ref[pl.ds(blk * tm, tm), :] = ys_ref[...]

    def fetch(r, carry):
        for k in range(TOP_K):
            r_ref[k, pl.ds(r, 1), :] = w_ref[pl.ds(wrow_ref[0, TOP_K * r + k], 1), :]
        return carry

    lax.fori_loop(0, tm, fetch, 0, unroll=8)
    route = route_ref[...]
    y = route[:, 2:3] * _unpack_halves(r_ref[0]) + route[:, 3:4] * _unpack_halves(r_ref[1])
    x2 = x1_ref[...] + mod_ref[5:6, :] * y
    o_ref[...] = _rms(x2) * gain_ref[...]


def _combine_final(win, x1, route, wrow3, ys, modtab, final_gain, tiles_per_batch, n_exp):
    n_tok, d = x1.shape
    tm = TOKEN_TILE
    n_assign = tm * TOP_K
    ys_specs = [pl.BlockSpec((tm, d // 2), lambda i, win, e=e, jj=jj: (win[i * n_exp + e] + jj, 0))
                for e in range(n_exp) for jj in range(2)]
    smem_blk = pl.BlockSpec((None, 1, n_assign), lambda i, win: (i, 0, 0), memory_space=pltpu.SMEM)
    return pl.pallas_call(
        functools.partial(_combine_kernel, n_exp=n_exp),
        grid_spec=pltpu.PrefetchScalarGridSpec(
            num_scalar_prefetch=1, grid=(n_tok // tm,),
            in_specs=[pl.BlockSpec((tm, d), lambda i, win: (i, 0)),
                      pl.BlockSpec((tm, LANES), lambda i, win: (i, 0)),
                      smem_blk, *ys_specs,
                      pl.BlockSpec((None, 6, d), lambda i, win: ((i // tiles_per_batch) * 2, 0, 0)),
                      pl.BlockSpec((1, d), lambda i, win: (0, 0))],
            out_specs=pl.BlockSpec((tm, d), lambda i, win: (i, 0)),
            scratch_shapes=[pltpu.VMEM((2 * n_exp * tm, d // 2), jnp.uint32),
                            pltpu.VMEM((TOP_K, tm, d // 2), jnp.uint32)]),
        out_shape=jax.ShapeDtypeStruct((n_tok, d), jnp.float32),
        compiler_params=_cparams(("arbitrary",)),
        name="moe_combine",
    )(win, x1, route, wrow3, *([ys] * (2 * n_exp)), modtab, final_gain.reshape(1, d))


def _moe_ffn_final(x1, h2p, route, w1, w3, w2, modtab, final_gain):
    b, n, d = x1.shape
    n_tok = b * n
    n_exp = w1.shape[0]
    n_tiles = n_tok * TOP_K // MOE_TILE + n_exp + 1
    eid = route[..., :TOP_K].astype(jnp.int32).reshape(n_tok * TOP_K)
    meta, tile_expert = _route_plan(eid, n_exp, n_tiles)
    sot3, wrow3, win = _slots(eid, meta[:n_exp], n_exp)
    tos = _invert(meta, sot3, n_exp, n_tiles * MOE_TILE)
    ys = _experts(tile_expert, meta[3 * n_exp:], tos.reshape(n_tiles, 1, MOE_TILE),
                  h2p, w1, w3, w2)
    out = _combine_final(win[:, :, 0].reshape(-1), x1.reshape(n_tok, d), route.reshape(n_tok, LANES),
                         wrow3, ys, modtab, final_gain, n // TOKEN_TILE, n_exp)
    return out.reshape(b, n, d)


def _rope_tables(n_lat, n_ctx):
    t = jnp.arange(n_lat, dtype=jnp.int32)
    n_freq = HEAD_DIM // 4
    inv_freq = ROPE_BASE ** (-jnp.arange(n_freq, dtype=jnp.float32) / n_freq)
    ang = jnp.concatenate([(t // GRID_W).astype(jnp.float32)[:, None] * inv_freq,
                           (t % GRID_W).astype(jnp.float32)[:, None] * inv_freq], axis=-1)
    cos, sin = jnp.cos(ang), jnp.sin(ang)
    reps = LANES // HEAD_DIM
    cos_t = jnp.tile(jnp.concatenate([cos, cos], axis=-1), (1, reps))
    sin_t = jnp.tile(jnp.concatenate([-sin, sin], axis=-1), (1, reps))
    cos_t = jnp.concatenate([cos_t, jnp.ones((n_ctx, LANES), jnp.float32)], axis=0)
    sin_t = jnp.concatenate([sin_t, jnp.zeros((n_ctx, LANES), jnp.float32)], axis=0)
    return cos_t, sin_t


def kernel(x, c, ctx, c_ctx, ada_w, ada_b, w_in, w_out, na_rpb, diff_lambda, diff_subln, conv_w,
           ffn_w1, ffn_w3, ffn_w2, router_w, moe_w1, moe_w3, moe_w2, final_gain):
    b, n, d = x.shape
    n_ctx = ctx.shape[1]
    depth = w_in.shape[0]
    assert d == D_MODEL and n % TOKEN_TILE == 0 and n_ctx == TOKEN_TILE and b + 1 <= 8
    assert depth == 2, "layer 0 dense with a context stream, layer 1 routed and final"
    bf = jnp.bfloat16

    stream = (x, ctx, 0)
    cvec = jnp.zeros((8, d), jnp.float32).at[:b].set(c).at[b].set(c_ctx)
    mod = _modulation(cvec, ada_w, ada_b).reshape(depth, 8, 6, d)
    cos_t, sin_t = _rope_tables(n, n_ctx)
    rows = n // GRID_W

    out = None
    for i in range(depth):
        lambda_init = 0.8 - 0.6 * math.exp(-0.3 * i)
        ctx_out = i < depth - 1
        modtab = jnp.stack([mod[i, :b], jnp.broadcast_to(mod[i, b], (b, 6, d))], axis=1).reshape(2 * b, 6, d)
        q, k, v, u, bg = _in_proj(*stream, modtab, cos_t, sin_t, w_in[i].astype(bf), n)
        bias = _na_bias_table(na_rpb[i], rows)
        na = _na_latent(q, k, v, bias, n)
        gain = diff_subln[i].reshape(1, 2 * HEAD_DIM)
        dif = _diff_attention(q, k, v, diff_lambda[i], gain, lambda_init,
                              q_rows=n, q_start=0, tq=1024, k_rows=n + n_ctx, k_start=0, tk=1408,
                              heads_per_step=4)
        if ctx_out:
            na_c = _na_context(q, k, v, n)
            dif_c = _diff_attention(q, k, v, diff_lambda[i], gain, lambda_init,
                                    q_rows=n_ctx, q_start=n, tq=n_ctx, k_rows=n_ctx, k_start=n, tk=n_ctx,
                                    heads_per_step=4)
            na = jnp.concatenate([na, na_c], axis=1)
            dif = jnp.concatenate([dif, dif_c], axis=1)
            m = i // 2
            x_all = _out_proj(*stream, na, dif, u, bg, conv_w[i], w_out[i].astype(bf), modtab, n, n + n_ctx,
                              ffn=(ffn_w1[m].astype(bf), ffn_w3[m].astype(bf), ffn_w2[m].astype(bf)))
            stream = (x_all, x_all, n // TOKEN_TILE)
        else:
            m = i // 2
            rw = jnp.zeros((d, LANES), jnp.float32).at[:, :N_EXPERTS].set(router_w[m])
            rw_hi = rw.astype(bf)
            rw = jnp.stack([rw_hi, (rw - rw_hi.astype(jnp.float32)).astype(bf)])
            x1, h2p, route = _out_proj(*stream, na, dif, u, bg, conv_w[i], w_out[i].astype(bf), modtab, n, n,
                                       router_w=rw)
            out = _moe_ffn_final(x1, h2p, route, moe_w1[m].astype(bf), moe_w3[m].astype(bf), moe_w2[m],
                                 modtab, final_gain)
    return out
```
